```python
import math
import jax, jax.numpy as jnp
from jax import lax
import numpy as np

D_MODEL = 1024
BATCH = 16
SEQ = 4096
DEPTH = 1
DEC_BATCH = 8
DEC_SEQ = 8192
PAST_LEN = 128

N_HEADS = 4
HEAD_DIM = 64
V_DIM = 2 * HEAD_DIM
ATTN_WIDTH = N_HEADS * V_DIM
QK_WIDTH = N_HEADS * 2 * HEAD_DIM
POOL_WINDOWS = (2, 4, 8, 16)
N_POOL_GROUPS = len(POOL_WINDOWS)
POOL_GROUP_DIM = 128
POOL_WIDTH = N_POOL_GROUPS * POOL_GROUP_DIM
MIX_WIDTH = ATTN_WIDTH + POOL_WIDTH
IN_WIDTH = 2 * QK_WIDTH + ATTN_WIDTH + POOL_WIDTH
D_FF = 2816
CONV_WIDTH = 3
ROPE_THETA = 10000.0
Q_BLOCK = 128
EPS = 1e-6

kernel_name = "hybrid_pool_diffattn_encoder"


def rms_norm(x, g):
    xf = x.astype(jnp.float32)
    y = xf * lax.rsqrt(jnp.mean(xf * xf, axis=-1, keepdims=True) + EPS)
    return (y * g.astype(jnp.float32)).astype(x.dtype)


def rope_tables(seq_len, dtype):
    inv = ROPE_THETA ** (-jnp.arange(0, HEAD_DIM, 2, dtype=jnp.float32) / HEAD_DIM)
    ang = jnp.arange(seq_len, dtype=jnp.float32)[:, None] * inv[None, :]
    ang = jnp.concatenate([ang, ang], axis=-1)
    return jnp.cos(ang).astype(dtype), jnp.sin(ang).astype(dtype)


def apply_rope(x, cos, sin):
    c = cos[None, :, None, None, :]
    s = sin[None, :, None, None, :]
    x1, x2 = jnp.split(x, 2, axis=-1)
    return x * c + jnp.concatenate([-x2, x1], axis=-1) * s


def diff_attention(q, k, v, lam):
    B, S = q.shape[0], q.shape[1]
    nb = S // Q_BLOCK
    scale = 1.0 / math.sqrt(HEAD_DIM)
    qb = jnp.moveaxis(q.reshape(B, nb, Q_BLOCK, N_HEADS, 2, HEAD_DIM), 1, 0)

    def one_block(qblk):
        s = jnp.einsum('bqhmd,bkhmd->bhmqk', qblk, k).astype(jnp.float32) * scale
        p = jax.nn.softmax(s, axis=-1)
        a = p[:, :, 0] - lam * p[:, :, 1]
        return jnp.einsum('bhqk,bkhe->bqhe', a.astype(v.dtype), v)

    o = lax.map(one_block, qb)
    return jnp.moveaxis(o, 0, 1).reshape(B, S, N_HEADS, V_DIM)


def multiscale_pool(p):
    B, S = p.shape[0], p.shape[1]
    pf = p.astype(jnp.float32)
    cs = jnp.concatenate([jnp.zeros((B, 1) + pf.shape[2:], jnp.float32),
                          jnp.cumsum(pf, axis=1)], axis=1)
    idx = jnp.arange(S)
    outs = []
    for g, w in enumerate(POOL_WINDOWS):
        lo = jnp.maximum(idx - w // 2, 0)
        hi = jnp.minimum(idx + w // 2 - 1, S - 1)
        csg = cs[:, :, g]
        tot = csg[:, hi + 1] - csg[:, lo]
        cnt = (hi - lo + 1).astype(jnp.float32)[None, :, None]
        outs.append(tot / cnt - pf[:, :, g])
    return jnp.stack(outs, axis=2).astype(p.dtype)


def depthwise_conv_centred(h, w, b):
    hp = jnp.pad(h, ((0, 0), (1, 1), (0, 0)))
    S = h.shape[1]
    return hp[:, 0:S] * w[0] + hp[:, 1:S + 1] * w[1] + hp[:, 2:S + 2] * w[2] + b


def encoder_layer(x, layer_idx, norm1_g, w_in, q_norm_g, k_norm_g, lambda_q1,
                  lambda_k1, lambda_q2, lambda_k2, subln_g, w_pool, pool_scale,
                  w_out, norm2_g, w_up, conv_w, conv_b, w_down):
    B, S, _ = x.shape
    lambda_init = 0.8 - 0.6 * math.exp(-0.3 * layer_idx)
    h = rms_norm(x, norm1_g)
    z = jnp.einsum('bsd,de->bse', h, w_in)
    zq, zk, zv, zp = jnp.split(
        z, [QK_WIDTH, 2 * QK_WIDTH, 2 * QK_WIDTH + ATTN_WIDTH], axis=-1)
    q = rms_norm(zq.reshape(B, S, N_HEADS, 2, HEAD_DIM), q_norm_g)
    k = rms_norm(zk.reshape(B, S, N_HEADS, 2, HEAD_DIM), k_norm_g)
    cos, sin = rope_tables(S, x.dtype)
    q = apply_rope(q, cos, sin)
    k = apply_rope(k, cos, sin)
    v = zv.reshape(B, S, N_HEADS, V_DIM)
    lam = (jnp.exp(jnp.sum(lambda_q1.astype(jnp.float32) * lambda_k1.astype(jnp.float32)))
           - jnp.exp(jnp.sum(lambda_q2.astype(jnp.float32) * lambda_k2.astype(jnp.float32)))
           + lambda_init)
    o = diff_attention(q, k, v, lam)
    o = rms_norm(o, subln_g) * (1.0 - lambda_init)
    o_attn = o.reshape(B, S, ATTN_WIDTH)
    pg = multiscale_pool(zp.reshape(B, S, N_POOL_GROUPS, POOL_GROUP_DIM))
    pg = jnp.einsum('bsgc,gce->bsge', pg, w_pool).reshape(B, S, POOL_WIDTH)
    o_pool = pg * pool_scale
    mixed = jnp.concatenate([o_attn, o_pool], axis=-1)
    x = x + jnp.einsum('bse,ed->bsd', mixed, w_out)
    h2 = rms_norm(x, norm2_g)
    u = jnp.einsum('bsd,df->bsf', h2, w_up)
    u = depthwise_conv_centred(u, conv_w, conv_b)
    gate, val = jnp.split(u, 2, axis=-1)
    x = x + jnp.einsum('bsf,fd->bsd', jax.nn.silu(gate) * val, w_down)
    return x


def setup_inputs(seed: int = 0) -> dict:
    key = jax.random.key(seed)
    ks = jax.random.split(key, 20)
    n = lambda k, shape, s: jax.random.normal(k, shape, jnp.float32) * s
    L = DEPTH
    return {
        "x_prompt": n(ks[0], (BATCH, SEQ, D_MODEL), 1.0),
        "x_sample": n(ks[1], (DEC_BATCH, DEC_SEQ, D_MODEL), 1.0),
        "norm1_g": 1.0 + n(ks[2], (L, D_MODEL), 0.02),
        "w_in": n(ks[3], (L, D_MODEL, IN_WIDTH), D_MODEL ** -0.5),
        "q_norm_g": 1.0 + n(ks[4], (L, HEAD_DIM), 0.02),
        "k_norm_g": 1.0 + n(ks[5], (L, HEAD_DIM), 0.02),
        "lambda_q1": n(ks[6], (L, HEAD_DIM), 0.1),
        "lambda_k1": n(ks[7], (L, HEAD_DIM), 0.1),
        "lambda_q2": n(ks[8], (L, HEAD_DIM), 0.1),
        "lambda_k2": n(ks[9], (L, HEAD_DIM), 0.1),
        "subln_g": 1.0 + n(ks[10], (L, V_DIM), 0.02),
        "w_pool": n(ks[11], (L, N_POOL_GROUPS, POOL_GROUP_DIM, POOL_GROUP_DIM), POOL_GROUP_DIM ** -0.5),
        "pool_scale": 1.0 + n(ks[12], (L, POOL_WIDTH), 0.1),
        "w_out": n(ks[13], (L, MIX_WIDTH, D_MODEL), MIX_WIDTH ** -0.5),
        "norm2_g": 1.0 + n(ks[14], (L, D_MODEL), 0.02),
        "w_up": n(ks[15], (L, D_MODEL, 2 * D_FF), D_MODEL ** -0.5),
        "conv_w": n(ks[16], (L, CONV_WIDTH, 2 * D_FF), CONV_WIDTH ** -0.5),
        "conv_b": n(ks[17], (L, 2 * D_FF), 0.01),
        "w_down": n(ks[18], (L, D_FF, D_MODEL), D_FF ** -0.5),
    }


def reference(x_prompt, x_sample, norm1_g, w_in, q_norm_g, k_norm_g, lambda_q1,
              lambda_k1, lambda_q2, lambda_k2, subln_g, w_pool, pool_scale,
              w_out, norm2_g, w_up, conv_w, conv_b, w_down):
    yp = x_prompt
    ys = x_sample
    for l in range(DEPTH):
        params = (norm1_g[l], w_in[l], q_norm_g[l], k_norm_g[l], lambda_q1[l],
                  lambda_k1[l], lambda_q2[l], lambda_k2[l], subln_g[l], w_pool[l],
                  pool_scale[l], w_out[l], norm2_g[l], w_up[l], conv_w[l],
                  conv_b[l], w_down[l])
        yp = encoder_layer(yp, l, *params)
        ys = encoder_layer(ys, l, *params)
    return (yp, ys)
```

```python
import functools
import math

import jax
import jax.numpy as jnp
from jax import lax
from jax.experimental import pallas as pl
from jax.experimental.pallas import tpu as pltpu

D_MODEL = 1024
N_HEADS = 4
HEAD_DIM = 64
V_DIM = 2 * HEAD_DIM
HEAD_WIDTH = 2 * HEAD_DIM
ATTN_WIDTH = N_HEADS * V_DIM
QK_WIDTH = N_HEADS * 2 * HEAD_DIM
POOL_WINDOWS = (2, 4, 8, 16)
POOL_GROUP_DIM = 128
POOL_WIDTH = len(POOL_WINDOWS) * POOL_GROUP_DIM
IN_WIDTH = 2 * QK_WIDTH + ATTN_WIDTH + POOL_WIDTH
D_FF = 2816
ROPE_THETA = 10000.0
EPS = 1e-6
LAMBDA_INIT = 0.8 - 0.6 * math.exp(-0.3 * 0)

HALO = 8
VMEM_LIMIT = 56 * 1024 * 1024

PRE_ROWS = 512
ATTN_Q = 256
ATTN_K = 512
MIX_ROWS = 512
FFN_ROWS = 512
FFN_CHUNK = 256


def _const_spec(shape):
    return pl.BlockSpec(shape, lambda *_: (0,) * len(shape))


def _rms(x, gain):
    return x * lax.rsqrt(jnp.mean(x * x, axis=-1, keepdims=True) + EPS) * gain


def _group_mean_square(x, gmat):
    sq = x * x
    hi = sq.astype(jnp.bfloat16)
    lo = (sq - hi.astype(jnp.float32)).astype(jnp.bfloat16)
    tot = (jnp.dot(hi, gmat, preferred_element_type=jnp.float32)
           + jnp.dot(lo, gmat, preferred_element_type=jnp.float32))
    return tot * (1.0 / HEAD_DIM)


def _norm_rope(x, gain, gmat, cos, sin_signed, first_half):
    xn = x * lax.rsqrt(_group_mean_square(x, gmat) + EPS) * gain
    rot = jnp.where(first_half,
                    pltpu.roll(xn, HEAD_WIDTH - HEAD_DIM // 2, axis=1),
                    pltpu.roll(xn, HEAD_DIM // 2, axis=1))
    return xn * cos + rot * sin_signed


def _pre_kernel(x_ref, g1_ref, win_ref, qg_ref, kg_ref, gmat_ref, cos_ref, sin_ref,
                qt_ref, k_ref, vt_ref, zp_ref):
    x = x_ref[0]
    h = _rms(x, g1_ref[...]).astype(jnp.bfloat16)
    z = jnp.dot(h, win_ref[...], preferred_element_type=jnp.float32)
    gmat = gmat_ref[...]
    cos = cos_ref[...]
    sin_signed = sin_ref[...]
    lane = lax.broadcasted_iota(jnp.int32, (1, HEAD_WIDTH), 1)
    first_half = (lane % HEAD_DIM) < (HEAD_DIM // 2)
    row = lax.broadcasted_iota(jnp.int32, (HEAD_WIDTH, 1), 0)
    scale = 1.0 / math.sqrt(HEAD_DIM)
    for hd in range(N_HEADS):
        lo = hd * HEAD_WIDTH
        q = _norm_rope(z[:, lo:lo + HEAD_WIDTH], qg_ref[...], gmat, cos, sin_signed, first_half)
        qt = (q * scale).T
        qt_ref[0, hd, 0] = jnp.where(row < HEAD_DIM, qt, 0.0).astype(jnp.bfloat16)
        qt_ref[0, hd, 1] = jnp.where(row >= HEAD_DIM, qt, 0.0).astype(jnp.bfloat16)
        k = _norm_rope(z[:, QK_WIDTH + lo:QK_WIDTH + lo + HEAD_WIDTH], kg_ref[...], gmat,
                       cos, sin_signed, first_half)
        k_ref[0, :, lo:lo + HEAD_WIDTH] = k.astype(jnp.bfloat16)
        v = z[:, 2 * QK_WIDTH + lo:2 * QK_WIDTH + lo + V_DIM]
        vt_ref[0, hd] = v.T.astype(jnp.bfloat16)
    zp_ref[0] = z[:, 2 * QK_WIDTH + ATTN_WIDTH:]


def _pre_call(x, g1, win, qg, kg, gmat, cos, sin_signed):
    B, S, _ = x.shape
    T = PRE_ROWS
    return pl.pallas_call(
        _pre_kernel,
        grid=(B, S // T),
        in_specs=[
            pl.BlockSpec((1, T, D_MODEL), lambda b, i: (b, i, 0)),
            _const_spec((1, D_MODEL)),
            _const_spec((D_MODEL, IN_WIDTH)),
            _const_spec((1, HEAD_WIDTH)),
            _const_spec((1, HEAD_WIDTH)),
            _const_spec((HEAD_WIDTH, HEAD_WIDTH)),
            pl.BlockSpec((T, HEAD_WIDTH), lambda b, i: (i, 0)),
            pl.BlockSpec((T, HEAD_WIDTH), lambda b, i: (i, 0)),
        ],
        out_specs=[
            pl.BlockSpec((1, N_HEADS, 2, HEAD_WIDTH, T), lambda b, i: (b, 0, 0, 0, i)),
            pl.BlockSpec((1, T, QK_WIDTH), lambda b, i: (b, i, 0)),
            pl.BlockSpec((1, N_HEADS, V_DIM, T), lambda b, i: (b, 0, 0, i)),
            pl.BlockSpec((1, T, POOL_WIDTH), lambda b, i: (b, i, 0)),
        ],
        out_shape=[
            jax.ShapeDtypeStruct((B, N_HEADS, 2, HEAD_WIDTH, S), jnp.bfloat16),
            jax.ShapeDtypeStruct((B, S, QK_WIDTH), jnp.bfloat16),
            jax.ShapeDtypeStruct((B, N_HEADS, V_DIM, S), jnp.bfloat16),
            jax.ShapeDtypeStruct((B, S, POOL_WIDTH), jnp.float32),
        ],
        compiler_params=pltpu.CompilerParams(
            dimension_semantics=("parallel", "parallel"), vmem_limit_bytes=VMEM_LIMIT),
        name="pre",
    )(x, g1, win, qg, kg, gmat, cos, sin_signed)


def _attn_kernel(lam_ref, qt_ref, k_ref, vt_ref, g_ref, o_ref, m_ref, l_ref, acc_ref):
    n_kt = k_ref.shape[1] // ATTN_K
    m_ref[...] = jnp.full(m_ref.shape, -jnp.inf, jnp.float32)
    l_ref[...] = jnp.zeros(l_ref.shape, jnp.float32)
    acc_ref[...] = jnp.zeros(acc_ref.shape, jnp.float32)

    def body(j, carry):
        start = pl.multiple_of(j * ATTN_K, ATTN_K)
        kt = k_ref[0, pl.ds(start, ATTN_K), :]
        vt = vt_ref[0, 0, :, pl.ds(start, ATTN_K)]
        for mp in range(2):
            s = jnp.dot(kt, qt_ref[0, 0, mp], preferred_element_type=jnp.float32)
            m_prev = m_ref[mp]
            m_new = jnp.maximum(m_prev, jnp.max(s, axis=0, keepdims=True))
            p = jnp.exp(s - m_new)
            alpha = jnp.exp(m_prev - m_new)
            l_ref[mp] = alpha * l_ref[mp] + jnp.sum(p, axis=0, keepdims=True)
            acc_ref[mp] = alpha * acc_ref[mp] + jnp.dot(
                vt, p.astype(jnp.bfloat16), preferred_element_type=jnp.float32)
            m_ref[mp] = m_new
        return carry

    lax.fori_loop(0, n_kt, body, 0)

    lam = lam_ref[0]
    o = acc_ref[0] / l_ref[0] - lam * (acc_ref[1] / l_ref[1])
    o = o * lax.rsqrt(jnp.mean(o * o, axis=0, keepdims=True) + EPS) * g_ref[...]
    o_ref[0] = (o * (1.0 - LAMBDA_INIT)).T.astype(o_ref.dtype)


def _attn_call(lam, qt, k, vt, subln_col):
    B, S, _ = k.shape
    return pl.pallas_call(
        _attn_kernel,
        grid=(B, N_HEADS, S // ATTN_Q),
        in_specs=[
            pl.BlockSpec(memory_space=pltpu.SMEM),
            pl.BlockSpec((1, 1, 2, HEAD_WIDTH, ATTN_Q), lambda b, h, i: (b, h, 0, 0, i)),
            pl.BlockSpec((1, S, HEAD_WIDTH), lambda b, h, i: (b, 0, h)),
            pl.BlockSpec((1, 1, V_DIM, S), lambda b, h, i: (b, h, 0, 0)),
            _const_spec((V_DIM, 1)),
        ],
        out_specs=pl.BlockSpec((1, ATTN_Q, V_DIM), lambda b, h, i: (b, i, h)),
        out_shape=jax.ShapeDtypeStruct((B, S, ATTN_WIDTH), jnp.bfloat16),
        scratch_shapes=[
            pltpu.VMEM((2, 1, ATTN_Q), jnp.float32),
            pltpu.VMEM((2, 1, ATTN_Q), jnp.float32),
            pltpu.VMEM((2, V_DIM, ATTN_Q), jnp.float32),
        ],
        compiler_params=pltpu.CompilerParams(
            dimension_semantics=("parallel", "parallel", "parallel"),
            vmem_limit_bytes=VMEM_LIMIT),
        name="attn",
    )(lam, qt, k, vt, subln_col)


def _mix_kernel(x_ref, oa_ref, zp_ref, zprev_ref, znext_ref, wpool_ref, pscale_ref, wout_ref,
                y_ref, ext_ref):
    T = MIX_ROWS
    i = pl.program_id(1)
    n_i = pl.num_programs(1)
    seq_len = n_i * T
    ext_ref[pl.ds(0, HALO), :] = jnp.where(i > 0, zprev_ref[0], 0.0)
    ext_ref[pl.ds(HALO, T), :] = zp_ref[0]
    ext_ref[pl.ds(HALO + T, HALO), :] = jnp.where(i < n_i - 1, znext_ref[0], 0.0)
    pos = i * T + lax.broadcasted_iota(jnp.int32, (T, 1), 0)

    pooled = []
    for g, w in enumerate(POOL_WINDOWS):
        c0 = g * POOL_GROUP_DIM
        tot = jnp.zeros((T, POOL_GROUP_DIM), jnp.float32)
        for off in range(-(w // 2), w // 2):
            tot = tot + ext_ref[pl.ds(HALO + off, T), c0:c0 + POOL_GROUP_DIM]
        lo = jnp.maximum(pos - w // 2, 0)
        hi = jnp.minimum(pos + w // 2 - 1, seq_len - 1)
        cnt = (hi - lo + 1).astype(jnp.float32)
        pg = tot / cnt - ext_ref[pl.ds(HALO, T), c0:c0 + POOL_GROUP_DIM]
        pw = jnp.dot(pg.astype(jnp.bfloat16), wpool_ref[g], preferred_element_type=jnp.float32)
        pooled.append((pw * pscale_ref[:, c0:c0 + POOL_GROUP_DIM]).astype(jnp.bfloat16))
    mixed = jnp.concatenate([oa_ref[0]] + pooled, axis=-1)
    y_ref[0] = x_ref[0] + jnp.dot(mixed, wout_ref[...], preferred_element_type=jnp.float32)


def _mix_call(x, o_attn, zp, wpool, pscale, wout):
    B, S, _ = x.shape
    T = MIX_ROWS
    nb = T // HALO
    last = S // HALO - 1
    return pl.pallas_call(
        _mix_kernel,
        grid=(B, S // T),
        in_specs=[
            pl.BlockSpec((1, T, D_MODEL), lambda b, i: (b, i, 0)),
            pl.BlockSpec((1, T, ATTN_WIDTH), lambda b, i: (b, i, 0)),
            pl.BlockSpec((1, T, POOL_WIDTH), lambda b, i: (b, i, 0)),
            pl.BlockSpec((1, HALO, POOL_WIDTH), lambda b, i: (b, jnp.maximum(i * nb - 1, 0), 0)),
            pl.BlockSpec((1, HALO, POOL_WIDTH), lambda b, i: (b, jnp.minimum((i + 1) * nb, last), 0)),
            _const_spec((len(POOL_WINDOWS), POOL_GROUP_DIM, POOL_GROUP_DIM)),
            _const_spec((1, POOL_WIDTH)),
            _const_spec((ATTN_WIDTH + POOL_WIDTH, D_MODEL)),
        ],
        out_specs=pl.BlockSpec((1, T, D_MODEL), lambda b, i: (b, i, 0)),
        out_shape=jax.ShapeDtypeStruct((B, S, D_MODEL), jnp.float32),
        scratch_shapes=[pltpu.VMEM((T + 2 * HALO, POOL_WIDTH), jnp.float32)],
        compiler_params=pltpu.CompilerParams(
            dimension_semantics=("parallel", "parallel"), vmem_limit_bytes=VMEM_LIMIT),
        name="mix",
    )(x, o_attn, zp, zp, zp, wpool, pscale, wout)


def _ffn_kernel(x_ref, xprev_ref, xnext_ref, g2_ref, wup_ref, cw_ref, cb_ref, wdown_ref,
                y_ref, h_ref, acc_ref):
    T = FFN_ROWS
    i = pl.program_id(1)
    n_i = pl.num_programs(1)
    g2 = g2_ref[...]
    h_ref[pl.ds(0, HALO), :] = _rms(jnp.where(i > 0, xprev_ref[0], 0.0), g2).astype(jnp.bfloat16)
    h_ref[pl.ds(HALO, T), :] = _rms(x_ref[0], g2).astype(jnp.bfloat16)
    h_ref[pl.ds(HALO + T, HALO), :] = _rms(
        jnp.where(i < n_i - 1, xnext_ref[0], 0.0), g2).astype(jnp.bfloat16)
    h = h_ref[...]

    def conv(u, c0):
        w = cw_ref[:, c0:c0 + FFN_CHUNK]
        return (u[HALO - 1:HALO - 1 + T] * w[0:1] + u[HALO:HALO + T] * w[1:2]
                + u[HALO + 1:HALO + 1 + T] * w[2:3] + cb_ref[:, c0:c0 + FFN_CHUNK])

    for c in range(D_FF // FFN_CHUNK):
        c0 = c * FFN_CHUNK
        ug = jnp.dot(h, wup_ref[:, c0:c0 + FFN_CHUNK], preferred_element_type=jnp.float32)
        uv = jnp.dot(h, wup_ref[:, D_FF + c0:D_FF + c0 + FFN_CHUNK],
                     preferred_element_type=jnp.float32)
        gate = conv(ug, c0)
        val = conv(uv, D_FF + c0)
        act = (gate / (1.0 + jnp.exp(-gate)) * val).astype(jnp.bfloat16)
        part = jnp.dot(act, wdown_ref[c0:c0 + FFN_CHUNK, :], preferred_element_type=jnp.float32)
        if c == 0:
            acc_ref[...] = part
        else:
            acc_ref[...] += part
    y_ref[0] = x_ref[0] + acc_ref[...]


def _ffn_call(x, g2, wup, cw, cb, wdown):
    B, S, _ = x.shape
    T = FFN_ROWS
    nb = T // HALO
    last = S // HALO - 1
    single = pl.Buffered(1)
    return pl.pallas_call(
        _ffn_kernel,
        grid=(B, S // T),
        in_specs=[
            pl.BlockSpec((1, T, D_MODEL), lambda b, i: (b, i, 0)),
            pl.BlockSpec((1, HALO, D_MODEL), lambda b, i: (b, jnp.maximum(i * nb - 1, 0), 0)),
            pl.BlockSpec((1, HALO, D_MODEL), lambda b, i: (b, jnp.minimum((i + 1) * nb, last), 0)),
            _const_spec((1, D_MODEL)),
            pl.BlockSpec((D_MODEL, 2 * D_FF), lambda b, i: (0, 0), pipeline_mode=single),
            _const_spec((3, 2 * D_FF)),
            _const_spec((1, 2 * D_FF)),
            pl.BlockSpec((D_FF, D_MODEL), lambda b, i: (0, 0), pipeline_mode=single),
        ],
        out_specs=pl.BlockSpec((1, T, D_MODEL), lambda b, i: (b, i, 0)),
        out_shape=jax.ShapeDtypeStruct((B, S, D_MODEL), jnp.float32),
        scratch_shapes=[
            pltpu.VMEM((T + 2 * HALO, D_MODEL), jnp.bfloat16),
            pltpu.VMEM((T, D_MODEL), jnp.float32),
        ],
        compiler_params=pltpu.CompilerParams(
            dimension_semantics=("parallel", "parallel"), vmem_limit_bytes=VMEM_LIMIT),
        name="ffn",
    )(x, x, x, g2, wup, cw, cb, wdown)


def _rope_tables(seq_len):
    inv = ROPE_THETA ** (-jnp.arange(0, HEAD_DIM, 2, dtype=jnp.float32) / HEAD_DIM)
    ang = jnp.arange(seq_len, dtype=jnp.float32)[:, None] * inv[None, :]
    ang = jnp.concatenate([ang, ang], axis=-1)
    cos = jnp.cos(ang)
    sin = jnp.sin(ang)
    half = HEAD_DIM // 2
    sin_signed = jnp.concatenate([-sin[:, :half], sin[:, half:]], axis=-1)
    return jnp.tile(cos, (1, 2)), jnp.tile(sin_signed, (1, 2))


def _layer(x, p):
    B, S, _ = x.shape
    cos, sin_signed = _rope_tables(S)
    qt, k, vt, zp = _pre_call(x, p["g1"], p["win"], p["qg"], p["kg"], p["gmat"], cos, sin_signed)
    o_attn = _attn_call(p["lam"], qt, k, vt, p["subln"])
    x1 = _mix_call(x, o_attn, zp, p["wpool"], p["pscale"], p["wout"])
    return _ffn_call(x1, p["g2"], p["wup"], p["cw"], p["cb"], p["wdown"])


def kernel(x_prompt, x_sample, norm1_g, w_in, q_norm_g, k_norm_g, lambda_q1, lambda_k1,
           lambda_q2, lambda_k2, subln_g, w_pool, pool_scale, w_out, norm2_g, w_up, conv_w,
           conv_b, w_down):
    f32 = jnp.float32
    bf16 = jnp.bfloat16
    group = jnp.arange(HEAD_WIDTH) // HEAD_DIM
    lam = (jnp.exp(jnp.sum(lambda_q1[0].astype(f32) * lambda_k1[0].astype(f32)))
           - jnp.exp(jnp.sum(lambda_q2[0].astype(f32) * lambda_k2[0].astype(f32)))
           + LAMBDA_INIT)
    p = {
        "g1": norm1_g[0].reshape(1, D_MODEL),
        "win": w_in[0].astype(bf16),
        "qg": jnp.tile(q_norm_g[0], 2).reshape(1, HEAD_WIDTH),
        "kg": jnp.tile(k_norm_g[0], 2).reshape(1, HEAD_WIDTH),
        "gmat": (group[:, None] == group[None, :]).astype(bf16),
        "lam": lam.reshape(1),
        "subln": subln_g[0].reshape(V_DIM, 1),
        "wpool": w_pool[0].astype(bf16),
        "pscale": pool_scale[0].reshape(1, POOL_WIDTH),
        "wout": w_out[0].astype(bf16),
        "g2": norm2_g[0].reshape(1, D_MODEL),
        "wup": w_up[0].astype(bf16),
        "cw": conv_w[0],
        "cb": conv_b[0].reshape(1, 2 * D_FF),
        "wdown": w_down[0].astype(bf16),
    }
    return (_layer(x_prompt, p), _layer(x_sample, p))
```

```python
import functools
import math

import jax
import jax.numpy as jnp
from jax import lax
from jax.experimental import pallas as pl
from jax.experimental.pallas import tpu as pltpu

D_MODEL = 1024
N_HEADS = 4
HEAD_DIM = 64
V_DIM = 2 * HEAD_DIM
HEAD_WIDTH = 2 * HEAD_DIM
ATTN_WIDTH = N_HEADS * V_DIM
QK_WIDTH = N_HEADS * 2 * HEAD_DIM
POOL_WINDOWS = (2, 4, 8, 16)
POOL_GROUP_DIM = 128
POOL_WIDTH = len(POOL_WINDOWS) * POOL_GROUP_DIM
IN_WIDTH = 2 * QK_WIDTH + ATTN_WIDTH + POOL_WIDTH
D_FF = 2816
ROPE_THETA = 10000.0
EPS = 1e-6
LAMBDA_INIT = 0.8 - 0.6 * math.exp(-0.3 * 0)
MAX_UNSHIFTED_LOGIT = 30.0

HALO = 8
VMEM_LIMIT = 56 * 1024 * 1024

PRE_ROWS = 512
ATTN_Q = 256
ATTN_K = 512
MIX_ROWS = 512
FFN_ROWS = 512
FFN_CHUNK = 256


def _const_spec(shape):
    return pl.BlockSpec(shape, lambda *_: (0,) * len(shape))


def _rms(x, gain):
    return x * lax.rsqrt(jnp.mean(x * x, axis=-1, keepdims=True) + EPS) * gain


def _group_mean_square(x, gmat):
    sq = x * x
    hi = sq.astype(jnp.bfloat16)
    lo = (sq - hi.astype(jnp.float32)).astype(jnp.bfloat16)
    tot = (jnp.dot(hi, gmat, preferred_element_type=jnp.float32)
           + jnp.dot(lo, gmat, preferred_element_type=jnp.float32))
    return tot * (1.0 / HEAD_DIM)


def _norm_rope(x, gain, gmat, cos, sin_signed, first_half):
    xn = x * lax.rsqrt(_group_mean_square(x, gmat) + EPS) * gain
    rot = jnp.where(first_half,
                    pltpu.roll(xn, HEAD_WIDTH - HEAD_DIM // 2, axis=1),
                    pltpu.roll(xn, HEAD_DIM // 2, axis=1))
    return xn * cos + rot * sin_signed


def _pre_kernel(x_ref, g1_ref, win_ref, qg_ref, kg_ref, gmat_ref, cos_ref, sin_ref,
                qt_ref, k_ref, vt_ref, zp_ref):
    x = x_ref[0]
    h = _rms(x, g1_ref[...]).astype(jnp.bfloat16)
    z = jnp.dot(h, win_ref[...], preferred_element_type=jnp.float32)
    gmat = gmat_ref[...]
    cos = cos_ref[...]
    sin_signed = sin_ref[...]
    lane = lax.broadcasted_iota(jnp.int32, (1, HEAD_WIDTH), 1)
    first_half = (lane % HEAD_DIM) < (HEAD_DIM // 2)
    row = lax.broadcasted_iota(jnp.int32, (HEAD_WIDTH, 1), 0)
    scale = 1.0 / math.sqrt(HEAD_DIM)
    for hd in range(N_HEADS):
        lo = hd * HEAD_WIDTH
        q = _norm_rope(z[:, lo:lo + HEAD_WIDTH], qg_ref[...], gmat, cos, sin_signed, first_half)
        qt = (q * scale).T
        qt_ref[0, hd, 0] = jnp.where(row < HEAD_DIM, qt, 0.0).astype(jnp.bfloat16)
        qt_ref[0, hd, 1] = jnp.where(row >= HEAD_DIM, qt, 0.0).astype(jnp.bfloat16)
        k = _norm_rope(z[:, QK_WIDTH + lo:QK_WIDTH + lo + HEAD_WIDTH], kg_ref[...], gmat,
                       cos, sin_signed, first_half)
        k_ref[0, :, lo:lo + HEAD_WIDTH] = k.astype(jnp.bfloat16)
        v = z[:, 2 * QK_WIDTH + lo:2 * QK_WIDTH + lo + V_DIM]
        vt_ref[0, hd] = v.T.astype(jnp.bfloat16)
    zp_ref[0] = z[:, 2 * QK_WIDTH + ATTN_WIDTH:]


def _pre_call(x, g1, win, qg, kg, gmat, cos, sin_signed):
    B, S, _ = x.shape
    T = PRE_ROWS
    return pl.pallas_call(
        _pre_kernel,
        grid=(B, S // T),
        in_specs=[
            pl.BlockSpec((1, T, D_MODEL), lambda b, i: (b, i, 0)),
            _const_spec((1, D_MODEL)),
            _const_spec((D_MODEL, IN_WIDTH)),
            _const_spec((1, HEAD_WIDTH)),
            _const_spec((1, HEAD_WIDTH)),
            _const_spec((HEAD_WIDTH, HEAD_WIDTH)),
            pl.BlockSpec((T, HEAD_WIDTH), lambda b, i: (i, 0)),
            pl.BlockSpec((T, HEAD_WIDTH), lambda b, i: (i, 0)),
        ],
        out_specs=[
            pl.BlockSpec((1, N_HEADS, 2, HEAD_WIDTH, T), lambda b, i: (b, 0, 0, 0, i)),
            pl.BlockSpec((1, T, QK_WIDTH), lambda b, i: (b, i, 0)),
            pl.BlockSpec((1, N_HEADS, V_DIM, T), lambda b, i: (b, 0, 0, i)),
            pl.BlockSpec((1, T, POOL_WIDTH), lambda b, i: (b, i, 0)),
        ],
        out_shape=[
            jax.ShapeDtypeStruct((B, N_HEADS, 2, HEAD_WIDTH, S), jnp.bfloat16),
            jax.ShapeDtypeStruct((B, S, QK_WIDTH), jnp.bfloat16),
            jax.ShapeDtypeStruct((B, N_HEADS, V_DIM, S), jnp.bfloat16),
            jax.ShapeDtypeStruct((B, S, POOL_WIDTH), jnp.float32),
        ],
        compiler_params=pltpu.CompilerParams(
            dimension_semantics=("parallel", "parallel"), vmem_limit_bytes=VMEM_LIMIT),
        name="pre",
    )(x, g1, win, qg, kg, gmat, cos, sin_signed)


def _attn_kernel(lam_ref, bounded_ref, qt_ref, k_ref, vt_ref, g_ref, o_ref,
                 m_ref, l_ref, l8_ref, acc_ref):
    n_kt = k_ref.shape[1] // ATTN_K
    acc_ref[...] = jnp.zeros(acc_ref.shape, jnp.float32)

    def tiles(j):
        start = pl.multiple_of(j * ATTN_K, ATTN_K)
        kt = k_ref[0, pl.ds(start, ATTN_K), :]
        vt = vt_ref[0, 0, :, pl.ds(start, ATTN_K)]
        return kt, vt

    @pl.when(bounded_ref[0] != 0)
    def _():
        l8_ref[...] = jnp.zeros(l8_ref.shape, jnp.float32)

        def body(j, carry):
            kt, vt = tiles(j)
            s = [jnp.dot(kt, qt_ref[0, 0, mp], preferred_element_type=jnp.float32)
                 for mp in range(2)]
            p = [jnp.exp(s[mp]) for mp in range(2)]
            for mp in range(2):
                l8_ref[mp] += jnp.sum(p[mp].reshape(ATTN_K // 8, 8, ATTN_Q), axis=0)
            for mp in range(2):
                acc_ref[mp] += jnp.dot(vt, p[mp].astype(jnp.bfloat16),
                                       preferred_element_type=jnp.float32)
            return carry

        lax.fori_loop(0, n_kt, body, 0, unroll=2)
        l_ref[...] = jnp.sum(l8_ref[...], axis=1, keepdims=True)

    @pl.when(bounded_ref[0] == 0)
    def _():
        m_ref[...] = jnp.full(m_ref.shape, -jnp.inf, jnp.float32)
        l_ref[...] = jnp.zeros(l_ref.shape, jnp.float32)

        def body(j, carry):
            kt, vt = tiles(j)
            for mp in range(2):
                s = jnp.dot(kt, qt_ref[0, 0, mp], preferred_element_type=jnp.float32)
                m_prev = m_ref[mp]
                m_new = jnp.maximum(m_prev, jnp.max(s, axis=0, keepdims=True))
                p = jnp.exp(s - m_new)
                alpha = jnp.exp(m_prev - m_new)
                l_ref[mp] = alpha * l_ref[mp] + jnp.sum(p, axis=0, keepdims=True)
                acc_ref[mp] = alpha * acc_ref[mp] + jnp.dot(
                    vt, p.astype(jnp.bfloat16), preferred_element_type=jnp.float32)
                m_ref[mp] = m_new
            return carry

        lax.fori_loop(0, n_kt, body, 0)

    lam = lam_ref[0]
    o = acc_ref[0] / l_ref[0] - lam * (acc_ref[1] / l_ref[1])
    o = o * lax.rsqrt(jnp.mean(o * o, axis=0, keepdims=True) + EPS) * g_ref[...]
    o_ref[0] = (o * (1.0 - LAMBDA_INIT)).T.astype(o_ref.dtype)


def _attn_call(lam, bounded, qt, k, vt, subln_col):
    B, S, _ = k.shape
    return pl.pallas_call(
        _attn_kernel,
        grid=(B, N_HEADS, S // ATTN_Q),
        in_specs=[
            pl.BlockSpec(memory_space=pltpu.SMEM),
            pl.BlockSpec(memory_space=pltpu.SMEM),
            pl.BlockSpec((1, 1, 2, HEAD_WIDTH, ATTN_Q), lambda b, h, i: (b, h, 0, 0, i)),
            pl.BlockSpec((1, S, HEAD_WIDTH), lambda b, h, i: (b, 0, h)),
            pl.BlockSpec((1, 1, V_DIM, S), lambda b, h, i: (b, h, 0, 0)),
            _const_spec((V_DIM, 1)),
        ],
        out_specs=pl.BlockSpec((1, ATTN_Q, V_DIM), lambda b, h, i: (b, i, h)),
        out_shape=jax.ShapeDtypeStruct((B, S, ATTN_WIDTH), jnp.bfloat16),
        scratch_shapes=[
            pltpu.VMEM((2, 1, ATTN_Q), jnp.float32),
            pltpu.VMEM((2, 1, ATTN_Q), jnp.float32),
            pltpu.VMEM((2, 8, ATTN_Q), jnp.float32),
            pltpu.VMEM((2, V_DIM, ATTN_Q), jnp.float32),
        ],
        compiler_params=pltpu.CompilerParams(
            dimension_semantics=("parallel", "parallel", "parallel"),
            vmem_limit_bytes=VMEM_LIMIT),
        name="attn",
    )(lam, bounded, qt, k, vt, subln_col)


def _mix_kernel(x_ref, oa_ref, zp_ref, zprev_ref, znext_ref, wpool_ref, pscale_ref, wout_ref,
                y_ref, ext_ref):
    T = MIX_ROWS
    i = pl.program_id(1)
    n_i = pl.num_programs(1)
    seq_len = n_i * T
    ext_ref[pl.ds(0, HALO), :] = jnp.where(i > 0, zprev_ref[0], 0.0)
    ext_ref[pl.ds(HALO, T), :] = zp_ref[0]
    ext_ref[pl.ds(HALO + T, HALO), :] = jnp.where(i < n_i - 1, znext_ref[0], 0.0)
    pos = i * T + lax.broadcasted_iota(jnp.int32, (T, 1), 0)

    pooled = []
    for g, w in enumerate(POOL_WINDOWS):
        c0 = g * POOL_GROUP_DIM
        tot = jnp.zeros((T, POOL_GROUP_DIM), jnp.float32)
        for off in range(-(w // 2), w // 2):
            tot = tot + ext_ref[pl.ds(HALO + off, T), c0:c0 + POOL_GROUP_DIM]
        lo = jnp.maximum(pos - w // 2, 0)
        hi = jnp.minimum(pos + w // 2 - 1, seq_len - 1)
        cnt = (hi - lo + 1).astype(jnp.float32)
        pg = tot / cnt - ext_ref[pl.ds(HALO, T), c0:c0 + POOL_GROUP_DIM]
        pw = jnp.dot(pg.astype(jnp.bfloat16), wpool_ref[g], preferred_element_type=jnp.float32)
        pooled.append((pw * pscale_ref[:, c0:c0 + POOL_GROUP_DIM]).astype(jnp.bfloat16))
    mixed = jnp.concatenate([oa_ref[0]] + pooled, axis=-1)
    y_ref[0] = x_ref[0] + jnp.dot(mixed, wout_ref[...], preferred_element_type=jnp.float32)


def _mix_call(x, o_attn, zp, wpool, pscale, wout):
    B, S, _ = x.shape
    T = MIX_ROWS
    nb = T // HALO
    last = S // HALO - 1
    return pl.pallas_call(
        _mix_kernel,
        grid=(B, S // T),
        in_specs=[
            pl.BlockSpec((1, T, D_MODEL), lambda b, i: (b, i, 0)),
            pl.BlockSpec((1, T, ATTN_WIDTH), lambda b, i: (b, i, 0)),
            pl.BlockSpec((1, T, POOL_WIDTH), lambda b, i: (b, i, 0)),
            pl.BlockSpec((1, HALO, POOL_WIDTH), lambda b, i: (b, jnp.maximum(i * nb - 1, 0), 0)),
            pl.BlockSpec((1, HALO, POOL_WIDTH), lambda b, i: (b, jnp.minimum((i + 1) * nb, last), 0)),
            _const_spec((len(POOL_WINDOWS), POOL_GROUP_DIM, POOL_GROUP_DIM)),
            _const_spec((1, POOL_WIDTH)),
            _const_spec((ATTN_WIDTH + POOL_WIDTH, D_MODEL)),
        ],
        out_specs=pl.BlockSpec((1, T, D_MODEL), lambda b, i: (b, i, 0)),
        out_shape=jax.ShapeDtypeStruct((B, S, D_MODEL), jnp.float32),
        scratch_shapes=[pltpu.VMEM((T + 2 * HALO, POOL_WIDTH), jnp.float32)],
        compiler_params=pltpu.CompilerParams(
            dimension_semantics=("parallel", "parallel"), vmem_limit_bytes=VMEM_LIMIT),
        name="mix",
    )(x, o_attn, zp, zp, zp, wpool, pscale, wout)


def _ffn_kernel(x_ref, xprev_ref, xnext_ref, g2_ref, wup_ref, cw_ref, cb_ref, wdown_ref,
                y_ref, h_ref, acc_ref):
    T = FFN_ROWS
    i = pl.program_id(1)
    n_i = pl.num_programs(1)
    g2 = g2_ref[...]
    h_ref[pl.ds(0, HALO), :] = _rms(jnp.where(i > 0, xprev_ref[0], 0.0), g2).astype(jnp.bfloat16)
    h_ref[pl.ds(HALO, T), :] = _rms(x_ref[0], g2).astype(jnp.bfloat16)
    h_ref[pl.ds(HALO + T, HALO), :] = _rms(
        jnp.where(i < n_i - 1, xnext_ref[0], 0.0), g2).astype(jnp.bfloat16)
    h = h_ref[...]

    def conv(u, c0):
        w = cw_ref[:, c0:c0 + FFN_CHUNK]
        return (u[HALO - 1:HALO - 1 + T] * w[0:1] + u[HALO:HALO + T] * w[1:2]
                + u[HALO + 1:HALO + 1 + T] * w[2:3] + cb_ref[:, c0:c0 + FFN_CHUNK])

    for c in range(D_FF // FFN_CHUNK):
        c0 = c * FFN_CHUNK
        ug = jnp.dot(h, wup_ref[:, c0:c0 + FFN_CHUNK], preferred_element_type=jnp.float32)
        uv = jnp.dot(h, wup_ref[:, D_FF + c0:D_FF + c0 + FFN_CHUNK],
                     preferred_element_type=jnp.float32)
        gate = conv(ug, c0)
        val = conv(uv, D_FF + c0)
        act = (gate / (1.0 + jnp.exp(-gate)) * val).astype(jnp.bfloat16)
        part = jnp.dot(act, wdown_ref[c0:c0 + FFN_CHUNK, :], preferred_element_type=jnp.float32)
        if c == 0:
            acc_ref[...] = part
        else:
            acc_ref[...] += part
    y_ref[0] = x_ref[0] + acc_ref[...]


def _ffn_call(x, g2, wup, cw, cb, wdown):
    B, S, _ = x.shape
    T = FFN_ROWS
    nb = T // HALO
    last = S // HALO - 1
    single = pl.Buffered(1)
    return pl.pallas_call(
        _ffn_kernel,
        grid=(B, S // T),
        in_specs=[
            pl.BlockSpec((1, T, D_MODEL), lambda b, i: (b, i, 0)),
            pl.BlockSpec((1, HALO, D_MODEL), lambda b, i: (b, jnp.maximum(i * nb - 1, 0), 0)),
            pl.BlockSpec((1, HALO, D_MODEL), lambda b, i: (b, jnp.minimum((i + 1) * nb, last), 0)),
            _const_spec((1, D_MODEL)),
            pl.BlockSpec((D_MODEL, 2 * D_FF), lambda b, i: (0, 0), pipeline_mode=single),
            _const_spec((3, 2 * D_FF)),
            _const_spec((1, 2 * D_FF)),
            pl.BlockSpec((D_FF, D_MODEL), lambda b, i: (0, 0), pipeline_mode=single),
        ],
        out_specs=pl.BlockSpec((1, T, D_MODEL), lambda b, i: (b, i, 0)),
        out_shape=jax.ShapeDtypeStruct((B, S, D_MODEL), jnp.float32),
        scratch_shapes=[
            pltpu.VMEM((T + 2 * HALO, D_MODEL), jnp.bfloat16),
            pltpu.VMEM((T, D_MODEL), jnp.float32),
        ],
        compiler_params=pltpu.CompilerParams(
            dimension_semantics=("parallel", "parallel"), vmem_limit_bytes=VMEM_LIMIT),
        name="ffn",
    )(x, x, x, g2, wup, cw, cb, wdown)


def _rope_tables(seq_len):
    inv = ROPE_THETA ** (-jnp.arange(0, HEAD_DIM, 2, dtype=jnp.float32) / HEAD_DIM)
    ang = jnp.arange(seq_len, dtype=jnp.float32)[:, None] * inv[None, :]
    ang = jnp.concatenate([ang, ang], axis=-1)
    cos = jnp.cos(ang)
    sin = jnp.sin(ang)
    half = HEAD_DIM // 2
    sin_signed = jnp.concatenate([-sin[:, :half], sin[:, half:]], axis=-1)
    return jnp.tile(cos, (1, 2)), jnp.tile(sin_signed, (1, 2))


def _layer(x, p):
    B, S, _ = x.shape
    cos, sin_signed = _rope_tables(S)
    qt, k, vt, zp = _pre_call(x, p["g1"], p["win"], p["qg"], p["kg"], p["gmat"], cos, sin_signed)
    o_attn = _attn_call(p["lam"], p["bounded"], qt, k, vt, p["subln"])
    x1 = _mix_call(x, o_attn, zp, p["wpool"], p["pscale"], p["wout"])
    return _ffn_call(x1, p["g2"], p["wup"], p["cw"], p["cb"], p["wdown"])


def kernel(x_prompt, x_sample, norm1_g, w_in, q_norm_g, k_norm_g, lambda_q1, lambda_k1,
           lambda_q2, lambda_k2, subln_g, w_pool, pool_scale, w_out, norm2_g, w_up, conv_w,
           conv_b, w_down):
    f32 = jnp.float32
    bf16 = jnp.bfloat16
    group = jnp.arange(HEAD_WIDTH) // HEAD_DIM
    lam = (jnp.exp(jnp.sum(lambda_q1[0].astype(f32) * lambda_k1[0].astype(f32)))
           - jnp.exp(jnp.sum(lambda_q2[0].astype(f32) * lambda_k2[0].astype(f32)))
           + LAMBDA_INIT)
    logit_bound = (1.01 * math.sqrt(HEAD_DIM) * jnp.max(jnp.abs(q_norm_g[0].astype(f32)))
                   * jnp.max(jnp.abs(k_norm_g[0].astype(f32))))
    p = {
        "g1": norm1_g[0].reshape(1, D_MODEL),
        "win": w_in[0].astype(bf16),
        "qg": jnp.tile(q_norm_g[0], 2).reshape(1, HEAD_WIDTH),
        "kg": jnp.tile(k_norm_g[0], 2).reshape(1, HEAD_WIDTH),
        "gmat": (group[:, None] == group[None, :]).astype(bf16),
        "lam": lam.reshape(1),
        "bounded": (logit_bound < MAX_UNSHIFTED_LOGIT).astype(jnp.int32).reshape(1),
        "subln": subln_g[0].reshape(V_DIM, 1),
        "wpool": w_pool[0].astype(bf16),
        "pscale": pool_scale[0].reshape(1, POOL_WIDTH),
        "wout": w_out[0].astype(bf16),
        "g2": norm2_g[0].reshape(1, D_MODEL),
        "wup": w_up[0].astype(bf16),
        "cw": conv_w[0],
        "cb": conv_b[0].reshape(1, 2 * D_FF),
        "wdown": w_down[0].astype(bf16),
    }
    return (_layer(x_prompt, p), _layer(x_sample, p))
```

```python
import functools
import math

import jax
import jax.numpy as jnp
from jax import lax
from jax.experimental import pallas as pl
from jax.experimental.pallas import tpu as pltpu

D_MODEL = 1024
N_HEADS = 4
HEAD_DIM = 64
V_DIM = 2 * HEAD_DIM
HEAD_WIDTH = 2 * HEAD_DIM
ATTN_WIDTH = N_HEADS * V_DIM
QK_WIDTH = N_HEADS * 2 * HEAD_DIM
POOL_WINDOWS = (2, 4, 8, 16)
POOL_GROUP_DIM = 128
POOL_WIDTH = len(POOL_WINDOWS) * POOL_GROUP_DIM
IN_WIDTH = 2 * QK_WIDTH + ATTN_WIDTH + POOL_WIDTH
D_FF = 2816
ROPE_THETA = 10000.0
EPS = 1e-6
LAMBDA_INIT = 0.8 - 0.6 * math.exp(-0.3 * 0)
MAX_UNSHIFTED_LOGIT = 30.0

HALO = 8
VMEM_LIMIT = 56 * 1024 * 1024

PRE_ROWS = 512
ATTN_Q = 256
ATTN_K = 512
ATTN_K_UNSHIFTED = 256
ATTN_GROUP = 32
MIX_ROWS = 512
FFN_ROWS = 512
FFN_CHUNK = 256


def _const_spec(shape):
    return pl.BlockSpec(shape, lambda *_: (0,) * len(shape))


def _rms(x, gain):
    return x * lax.rsqrt(jnp.mean(x * x, axis=-1, keepdims=True) + EPS) * gain


def _group_mean_square(x, gmat):
    sq = x * x
    hi = sq.astype(jnp.bfloat16)
    lo = (sq - hi.astype(jnp.float32)).astype(jnp.bfloat16)
    tot = (jnp.dot(hi, gmat, preferred_element_type=jnp.float32)
           + jnp.dot(lo, gmat, preferred_element_type=jnp.float32))
    return tot * (1.0 / HEAD_DIM)


def _norm_rope(x, gain, gmat, cos, sin_signed, first_half):
    xn = x * lax.rsqrt(_group_mean_square(x, gmat) + EPS) * gain
    rot = jnp.where(first_half,
                    pltpu.roll(xn, HEAD_WIDTH - HEAD_DIM // 2, axis=1),
                    pltpu.roll(xn, HEAD_DIM // 2, axis=1))
    return xn * cos + rot * sin_signed


def _pre_kernel(x_ref, g1_ref, win_ref, qg_ref, kg_ref, gmat_ref, cos_ref, sin_ref,
                qt_ref, k_ref, vt_ref, zp_ref):
    x = x_ref[0]
    h = _rms(x, g1_ref[...]).astype(jnp.bfloat16)
    z = jnp.dot(h, win_ref[...], preferred_element_type=jnp.float32)
    gmat = gmat_ref[...]
    cos = cos_ref[...]
    sin_signed = sin_ref[...]
    lane = lax.broadcasted_iota(jnp.int32, (1, HEAD_WIDTH), 1)
    first_half = (lane % HEAD_DIM) < (HEAD_DIM // 2)
    row = lax.broadcasted_iota(jnp.int32, (HEAD_WIDTH, 1), 0)
    scale = 1.0 / math.sqrt(HEAD_DIM)
    for hd in range(N_HEADS):
        lo = hd * HEAD_WIDTH
        q = _norm_rope(z[:, lo:lo + HEAD_WIDTH], qg_ref[...], gmat, cos, sin_signed, first_half)
        qt = (q * scale).T
        qt_ref[0, hd, 0] = jnp.where(row < HEAD_DIM, qt, 0.0).astype(jnp.bfloat16)
        qt_ref[0, hd, 1] = jnp.where(row >= HEAD_DIM, qt, 0.0).astype(jnp.bfloat16)
        k = _norm_rope(z[:, QK_WIDTH + lo:QK_WIDTH + lo + HEAD_WIDTH], kg_ref[...], gmat,
                       cos, sin_signed, first_half)
        k_ref[0, :, lo:lo + HEAD_WIDTH] = k.astype(jnp.bfloat16)
        v = z[:, 2 * QK_WIDTH + lo:2 * QK_WIDTH + lo + V_DIM]
        vt_ref[0, hd] = v.T.astype(jnp.bfloat16)
    zp_ref[0] = z[:, 2 * QK_WIDTH + ATTN_WIDTH:]


def _pre_call(x, g1, win, qg, kg, gmat, cos, sin_signed):
    B, S, _ = x.shape
    T = PRE_ROWS
    return pl.pallas_call(
        _pre_kernel,
        grid=(B, S // T),
        in_specs=[
            pl.BlockSpec((1, T, D_MODEL), lambda b, i: (b, i, 0)),
            _const_spec((1, D_MODEL)),
            _const_spec((D_MODEL, IN_WIDTH)),
            _const_spec((1, HEAD_WIDTH)),
            _const_spec((1, HEAD_WIDTH)),
            _const_spec((HEAD_WIDTH, HEAD_WIDTH)),
            pl.BlockSpec((T, HEAD_WIDTH), lambda b, i: (i, 0)),
            pl.BlockSpec((T, HEAD_WIDTH), lambda b, i: (i, 0)),
        ],
        out_specs=[
            pl.BlockSpec((1, N_HEADS, 2, HEAD_WIDTH, T), lambda b, i: (b, 0, 0, 0, i)),
            pl.BlockSpec((1, T, QK_WIDTH), lambda b, i: (b, i, 0)),
            pl.BlockSpec((1, N_HEADS, V_DIM, T), lambda b, i: (b, 0, 0, i)),
            pl.BlockSpec((1, T, POOL_WIDTH), lambda b, i: (b, i, 0)),
        ],
        out_shape=[
            jax.ShapeDtypeStruct((B, N_HEADS, 2, HEAD_WIDTH, S), jnp.bfloat16),
            jax.ShapeDtypeStruct((B, S, QK_WIDTH), jnp.bfloat16),
            jax.ShapeDtypeStruct((B, N_HEADS, V_DIM, S), jnp.bfloat16),
            jax.ShapeDtypeStruct((B, S, POOL_WIDTH), jnp.float32),
        ],
        compiler_params=pltpu.CompilerParams(
            dimension_semantics=("parallel", "parallel"), vmem_limit_bytes=VMEM_LIMIT),
        name="pre",
    )(x, g1, win, qg, kg, gmat, cos, sin_signed)


def _attn_kernel(lam_ref, bounded_ref, qt_ref, k_ref, vt_ref, g_ref, o_ref,
                 m_ref, l_ref, l8_ref, acc_ref):
    n_kt = k_ref.shape[1] // ATTN_K
    acc_ref[...] = jnp.zeros(acc_ref.shape, jnp.float32)

    def tiles(j):
        start = pl.multiple_of(j * ATTN_K, ATTN_K)
        kt = k_ref[0, pl.ds(start, ATTN_K), :]
        vt = vt_ref[0, 0, :, pl.ds(start, ATTN_K)]
        return kt, vt

    @pl.when(bounded_ref[0] != 0)
    def _():
        l8_ref[...] = jnp.zeros(l8_ref.shape, jnp.float32)

        tk = ATTN_K_UNSHIFTED
        group = min(ATTN_GROUP, k_ref.shape[1] // tk)

        def scores(start):
            kt = k_ref[0, pl.ds(start, tk), :]
            return [jnp.dot(kt, qt_ref[0, 0, mp], preferred_element_type=jnp.float32)
                    for mp in range(2)]

        def body(j, carry):
            base = pl.multiple_of(j * (group * tk), group * tk)
            l8 = [None, None]
            acc = [None, None]
            s = scores(base)
            for t in range(group):
                s_next = scores(base + (t + 1) * tk) if t + 1 < group else None
                vt = vt_ref[0, 0, :, pl.ds(base + t * tk, tk)]
                for mp in range(2):
                    p = jnp.exp(s[mp])
                    psum = jnp.sum(p.reshape(tk // 8, 8, ATTN_Q), axis=0)
                    pv = jnp.dot(vt, p.astype(jnp.bfloat16), preferred_element_type=jnp.float32)
                    l8[mp] = psum if l8[mp] is None else l8[mp] + psum
                    acc[mp] = pv if acc[mp] is None else acc[mp] + pv
                s = s_next
            for mp in range(2):
                l8_ref[mp] += l8[mp]
                acc_ref[mp] += acc[mp]
            return carry

        lax.fori_loop(0, k_ref.shape[1] // (group * tk), body, 0)
        l_ref[...] = jnp.sum(l8_ref[...], axis=1, keepdims=True)

    @pl.when(bounded_ref[0] == 0)
    def _():
        m_ref[...] = jnp.full(m_ref.shape, -jnp.inf, jnp.float32)
        l_ref[...] = jnp.zeros(l_ref.shape, jnp.float32)

        def body(j, carry):
            kt, vt = tiles(j)
            for mp in range(2):
                s = jnp.dot(kt, qt_ref[0, 0, mp], preferred_element_type=jnp.float32)
                m_prev = m_ref[mp]
                m_new = jnp.maximum(m_prev, jnp.max(s, axis=0, keepdims=True))
                p = jnp.exp(s - m_new)
                alpha = jnp.exp(m_prev - m_new)
                l_ref[mp] = alpha * l_ref[mp] + jnp.sum(p, axis=0, keepdims=True)
                acc_ref[mp] = alpha * acc_ref[mp] + jnp.dot(
                    vt, p.astype(jnp.bfloat16), preferred_element_type=jnp.float32)
                m_ref[mp] = m_new
            return carry

        lax.fori_loop(0, n_kt, body, 0)

    lam = lam_ref[0]
    o = acc_ref[0] / l_ref[0] - lam * (acc_ref[1] / l_ref[1])
    o = o * lax.rsqrt(jnp.mean(o * o, axis=0, keepdims=True) + EPS) * g_ref[...]
    o_ref[0] = (o * (1.0 - LAMBDA_INIT)).T.astype(o_ref.dtype)


def _attn_call(lam, bounded, qt, k, vt, subln_col):
    B, S, _ = k.shape
    return pl.pallas_call(
        _attn_kernel,
        grid=(B, N_HEADS, S // ATTN_Q),
        in_specs=[
            pl.BlockSpec(memory_space=pltpu.SMEM),
            pl.BlockSpec(memory_space=pltpu.SMEM),
            pl.BlockSpec((1, 1, 2, HEAD_WIDTH, ATTN_Q), lambda b, h, i: (b, h, 0, 0, i)),
            pl.BlockSpec((1, S, HEAD_WIDTH), lambda b, h, i: (b, 0, h)),
            pl.BlockSpec((1, 1, V_DIM, S), lambda b, h, i: (b, h, 0, 0)),
            _const_spec((V_DIM, 1)),
        ],
        out_specs=pl.BlockSpec((1, ATTN_Q, V_DIM), lambda b, h, i: (b, i, h)),
        out_shape=jax.ShapeDtypeStruct((B, S, ATTN_WIDTH), jnp.bfloat16),
        scratch_shapes=[
            pltpu.VMEM((2, 1, ATTN_Q), jnp.float32),
            pltpu.VMEM((2, 1, ATTN_Q), jnp.float32),
            pltpu.VMEM((2, 8, ATTN_Q), jnp.float32),
            pltpu.VMEM((2, V_DIM, ATTN_Q), jnp.float32),
        ],
        compiler_params=pltpu.CompilerParams(
            dimension_semantics=("parallel", "parallel", "parallel"),
            vmem_limit_bytes=VMEM_LIMIT),
        name="attn",
    )(lam, bounded, qt, k, vt, subln_col)


def _mix_kernel(x_ref, oa_ref, zp_ref, zprev_ref, znext_ref, wpool_ref, pscale_ref, wout_ref,
                y_ref, ext_ref):
    T = MIX_ROWS
    i = pl.program_id(1)
    n_i = pl.num_programs(1)
    seq_len = n_i * T
    ext_ref[pl.ds(0, HALO), :] = jnp.where(i > 0, zprev_ref[0], 0.0)
    ext_ref[pl.ds(HALO, T), :] = zp_ref[0]
    ext_ref[pl.ds(HALO + T, HALO), :] = jnp.where(i < n_i - 1, znext_ref[0], 0.0)
    pos = i * T + lax.broadcasted_iota(jnp.int32, (T, 1), 0)

    pooled = []
    for g, w in enumerate(POOL_WINDOWS):
        c0 = g * POOL_GROUP_DIM
        tot = jnp.zeros((T, POOL_GROUP_DIM), jnp.float32)
        for off in range(-(w // 2), w // 2):
            tot = tot + ext_ref[pl.ds(HALO + off, T), c0:c0 + POOL_GROUP_DIM]
        lo = jnp.maximum(pos - w // 2, 0)
        hi = jnp.minimum(pos + w // 2 - 1, seq_len - 1)
        cnt = (hi - lo + 1).astype(jnp.float32)
        pg = tot / cnt - ext_ref[pl.ds(HALO, T), c0:c0 + POOL_GROUP_DIM]
        pw = jnp.dot(pg.astype(jnp.bfloat16), wpool_ref[g], preferred_element_type=jnp.float32)
        pooled.append((pw * pscale_ref[:, c0:c0 + POOL_GROUP_DIM]).astype(jnp.bfloat16))
    mixed = jnp.concatenate([oa_ref[0]] + pooled, axis=-1)
    y_ref[0] = x_ref[0] + jnp.dot(mixed, wout_ref[...], preferred_element_type=jnp.float32)


def _mix_call(x, o_attn, zp, wpool, pscale, wout):
    B, S, _ = x.shape
    T = MIX_ROWS
    nb = T // HALO
    last = S // HALO - 1
    return pl.pallas_call(
        _mix_kernel,
        grid=(B, S // T),
        in_specs=[
            pl.BlockSpec((1, T, D_MODEL), lambda b, i: (b, i, 0)),
            pl.BlockSpec((1, T, ATTN_WIDTH), lambda b, i: (b, i, 0)),
            pl.BlockSpec((1, T, POOL_WIDTH), lambda b, i: (b, i, 0)),
            pl.BlockSpec((1, HALO, POOL_WIDTH), lambda b, i: (b, jnp.maximum(i * nb - 1, 0), 0)),
            pl.BlockSpec((1, HALO, POOL_WIDTH), lambda b, i: (b, jnp.minimum((i + 1) * nb, last), 0)),
            _const_spec((len(POOL_WINDOWS), POOL_GROUP_DIM, POOL_GROUP_DIM)),
            _const_spec((1, POOL_WIDTH)),
            _const_spec((ATTN_WIDTH + POOL_WIDTH, D_MODEL)),
        ],
        out_specs=pl.BlockSpec((1, T, D_MODEL), lambda b, i: (b, i, 0)),
        out_shape=jax.ShapeDtypeStruct((B, S, D_MODEL), jnp.float32),
        scratch_shapes=[pltpu.VMEM((T + 2 * HALO, POOL_WIDTH), jnp.float32)],
        compiler_params=pltpu.CompilerParams(
            dimension_semantics=("parallel", "parallel"), vmem_limit_bytes=VMEM_LIMIT),
        name="mix",
    )(x, o_attn, zp, zp, zp, wpool, pscale, wout)


def _ffn_kernel(x_ref, xprev_ref, xnext_ref, g2_ref, wup_ref, cw_ref, cb_ref, wdown_ref,
                y_ref, h_ref, acc_ref):
    T = FFN_ROWS
    i = pl.program_id(1)
    n_i = pl.num_programs(1)
    g2 = g2_ref[...]
    h_ref[pl.ds(0, HALO), :] = _rms(jnp.where(i > 0, xprev_ref[0], 0.0), g2).astype(jnp.bfloat16)
    h_ref[pl.ds(HALO, T), :] = _rms(x_ref[0], g2).astype(jnp.bfloat16)
    h_ref[pl.ds(HALO + T, HALO), :] = _rms(
        jnp.where(i < n_i - 1, xnext_ref[0], 0.0), g2).astype(jnp.bfloat16)
    h = h_ref[...]

    def conv(u, c0):
        w = cw_ref[:, c0:c0 + FFN_CHUNK]
        return (u[HALO - 1:HALO - 1 + T] * w[0:1] + u[HALO:HALO + T] * w[1:2]
                + u[HALO + 1:HALO + 1 + T] * w[2:3] + cb_ref[:, c0:c0 + FFN_CHUNK])

    for c in range(D_FF // FFN_CHUNK):
        c0 = c * FFN_CHUNK
        ug = jnp.dot(h, wup_ref[:, c0:c0 + FFN_CHUNK], preferred_element_type=jnp.float32)
        uv = jnp.dot(h, wup_ref[:, D_FF + c0:D_FF + c0 + FFN_CHUNK],
                     preferred_element_type=jnp.float32)
        gate = conv(ug, c0)
        val = conv(uv, D_FF + c0)
        act = (gate / (1.0 + jnp.exp(-gate)) * val).astype(jnp.bfloat16)
        part = jnp.dot(act, wdown_ref[c0:c0 + FFN_CHUNK, :], preferred_element_type=jnp.float32)
        if c == 0:
            acc_ref[...] = part
        else:
            acc_ref[...] += part
    y_ref[0] = x_ref[0] + acc_ref[...]


def _ffn_call(x, g2, wup, cw, cb, wdown):
    B, S, _ = x.shape
    T = FFN_ROWS
    nb = T // HALO
    last = S // HALO - 1
    single = pl.Buffered(1)
    return pl.pallas_call(
        _ffn_kernel,
        grid=(B, S // T),
        in_specs=[
            pl.BlockSpec((1, T, D_MODEL), lambda b, i: (b, i, 0)),
            pl.BlockSpec((1, HALO, D_MODEL), lambda b, i: (b, jnp.maximum(i * nb - 1, 0), 0)),
            pl.BlockSpec((1, HALO, D_MODEL), lambda b, i: (b, jnp.minimum((i + 1) * nb, last), 0)),
            _const_spec((1, D_MODEL)),
            pl.BlockSpec((D_MODEL, 2 * D_FF), lambda b, i: (0, 0), pipeline_mode=single),
            _const_spec((3, 2 * D_FF)),
            _const_spec((1, 2 * D_FF)),
            pl.BlockSpec((D_FF, D_MODEL), lambda b, i: (0, 0), pipeline_mode=single),
        ],
        out_specs=pl.BlockSpec((1, T, D_MODEL), lambda b, i: (b, i, 0)),
        out_shape=jax.ShapeDtypeStruct((B, S, D_MODEL), jnp.float32),
        scratch_shapes=[
            pltpu.VMEM((T + 2 * HALO, D_MODEL), jnp.bfloat16),
            pltpu.VMEM((T, D_MODEL), jnp.float32),
        ],
        compiler_params=pltpu.CompilerParams(
            dimension_semantics=("parallel", "parallel"), vmem_limit_bytes=VMEM_LIMIT),
        name="ffn",
    )(x, x, x, g2, wup, cw, cb, wdown)


def _rope_tables(seq_len):
    inv = ROPE_THETA ** (-jnp.arange(0, HEAD_DIM, 2, dtype=jnp.float32) / HEAD_DIM)
    ang = jnp.arange(seq_len, dtype=jnp.float32)[:, None] * inv[None, :]
    ang = jnp.concatenate([ang, ang], axis=-1)
    cos = jnp.cos(ang)
    sin = jnp.sin(ang)
    half = HEAD_DIM // 2
    sin_signed = jnp.concatenate([-sin[:, :half], sin[:, half:]], axis=-1)
    return jnp.tile(cos, (1, 2)), jnp.tile(sin_signed, (1, 2))


def _layer(x, p):
    B, S, _ = x.shape
    cos, sin_signed = _rope_tables(S)
    qt, k, vt, zp = _pre_call(x, p["g1"], p["win"], p["qg"], p["kg"], p["gmat"], cos, sin_signed)
    o_attn = _attn_call(p["lam"], p["bounded"], qt, k, vt, p["subln"])
    x1 = _mix_call(x, o_attn, zp, p["wpool"], p["pscale"], p["wout"])
    return _ffn_call(x1, p["g2"], p["wup"], p["cw"], p["cb"], p["wdown"])


def kernel(x_prompt, x_sample, norm1_g, w_in, q_norm_g, k_norm_g, lambda_q1, lambda_k1,
           lambda_q2, lambda_k2, subln_g, w_pool, pool_scale, w_out, norm2_g, w_up, conv_w,
           conv_b, w_down):
    f32 = jnp.float32
    bf16 = jnp.bfloat16
    group = jnp.arange(HEAD_WIDTH) // HEAD_DIM
    lam = (jnp.exp(jnp.sum(lambda_q1[0].astype(f32) * lambda_k1[0].astype(f32)))
           - jnp.exp(jnp.sum(lambda_q2[0].astype(f32) * lambda_k2[0].astype(f32)))
           + LAMBDA_INIT)
    logit_bound = (1.01 * math.sqrt(HEAD_DIM) * jnp.max(jnp.abs(q_norm_g[0].astype(f32)))
                   * jnp.max(jnp.abs(k_norm_g[0].astype(f32))))
    p = {
        "g1": norm1_g[0].reshape(1, D_MODEL),
        "win": w_in[0].astype(bf16),
        "qg": jnp.tile(q_norm_g[0], 2).reshape(1, HEAD_WIDTH),
        "kg": jnp.tile(k_norm_g[0], 2).reshape(1, HEAD_WIDTH),
        "gmat": (group[:, None] == group[None, :]).astype(bf16),
        "lam": lam.reshape(1),
        "bounded": (logit_bound < MAX_UNSHIFTED_LOGIT).astype(jnp.int32).reshape(1),
        "subln": subln_g[0].reshape(V_DIM, 1),
        "wpool": w_pool[0].astype(bf16),
        "pscale": pool_scale[0].reshape(1, POOL_WIDTH),
        "wout": w_out[0].astype(bf16),
        "g2": norm2_g[0].reshape(1, D_MODEL),
        "wup": w_up[0].astype(bf16),
        "cw": conv_w[0],
        "cb": conv_b[0].reshape(1, 2 * D_FF),
        "wdown": w_down[0].astype(bf16),
    }
    return (_layer(x_prompt, p), _layer(x_sample, p))
```

```python
import functools
import math

import jax
import jax.numpy as jnp
from jax import lax
from jax.experimental import pallas as pl
from jax.experimental.pallas import tpu as pltpu

D_MODEL = 1024
N_HEADS = 4
HEAD_DIM = 64
V_DIM = 2 * HEAD_DIM
HEAD_WIDTH = 2 * HEAD_DIM
ATTN_WIDTH = N_HEADS * V_DIM
QK_WIDTH = N_HEADS * 2 * HEAD_DIM
POOL_WINDOWS = (2, 4, 8, 16)
POOL_GROUP_DIM = 128
POOL_WIDTH = len(POOL_WINDOWS) * POOL_GROUP_DIM
IN_WIDTH = 2 * QK_WIDTH + ATTN_WIDTH + POOL_WIDTH
D_FF = 2816
ROPE_THETA = 10000.0
EPS = 1e-6
LAMBDA_INIT = 0.8 - 0.6 * math.exp(-0.3 * 0)
MAX_UNSHIFTED_LOGIT = 30.0

HALO = 8
VMEM_LIMIT = 56 * 1024 * 1024

PRE_ROWS = 512
ATTN_Q = 256
ATTN_K = 512
ATTN_UNROLLED_TILES = 64
ATTN_K_UNSHIFTED = 256
MIX_ROWS = 512
FFN_ROWS = 512
FFN_CHUNK = 256


def _const_spec(shape):
    return pl.BlockSpec(shape, lambda *_: (0,) * len(shape))


def _rms(x, gain):
    return x * lax.rsqrt(jnp.mean(x * x, axis=-1, keepdims=True) + EPS) * gain


def _group_mean_square(x, gmat):
    sq = x * x
    hi = sq.astype(jnp.bfloat16)
    lo = (sq - hi.astype(jnp.float32)).astype(jnp.bfloat16)
    tot = (jnp.dot(hi, gmat, preferred_element_type=jnp.float32)
           + jnp.dot(lo, gmat, preferred_element_type=jnp.float32))
    return tot * (1.0 / HEAD_DIM)


def _norm_rope(x, gain, gmat, cos, sin_signed, first_half):
    xn = x * lax.rsqrt(_group_mean_square(x, gmat) + EPS) * gain
    rot = jnp.where(first_half,
                    pltpu.roll(xn, HEAD_WIDTH - HEAD_DIM // 2, axis=1),
                    pltpu.roll(xn, HEAD_DIM // 2, axis=1))
    return xn * cos + rot * sin_signed


def _pre_kernel(x_ref, g1_ref, win_ref, qg_ref, kg_ref, gmat_ref, cos_ref, sin_ref,
                qt_ref, k_ref, vt_ref, zp_ref):
    x = x_ref[0]
    h = _rms(x, g1_ref[...]).astype(jnp.bfloat16)
    z = jnp.dot(h, win_ref[...], preferred_element_type=jnp.float32)
    gmat = gmat_ref[...]
    cos = cos_ref[...]
    sin_signed = sin_ref[...]
    lane = lax.broadcasted_iota(jnp.int32, (1, HEAD_WIDTH), 1)
    first_half = (lane % HEAD_DIM) < (HEAD_DIM // 2)
    row = lax.broadcasted_iota(jnp.int32, (HEAD_WIDTH, 1), 0)
    scale = 1.0 / math.sqrt(HEAD_DIM)
    for hd in range(N_HEADS):
        lo = hd * HEAD_WIDTH
        q = _norm_rope(z[:, lo:lo + HEAD_WIDTH], qg_ref[...], gmat, cos, sin_signed, first_half)
        qt = (q * scale).T
        qt_ref[0, hd, 0] = jnp.where(row < HEAD_DIM, qt, 0.0).astype(jnp.bfloat16)
        qt_ref[0, hd, 1] = jnp.where(row >= HEAD_DIM, qt, 0.0).astype(jnp.bfloat16)
        k = _norm_rope(z[:, QK_WIDTH + lo:QK_WIDTH + lo + HEAD_WIDTH], kg_ref[...], gmat,
                       cos, sin_signed, first_half)
        k_ref[0, :, lo:lo + HEAD_WIDTH] = k.astype(jnp.bfloat16)
        v = z[:, 2 * QK_WIDTH + lo:2 * QK_WIDTH + lo + V_DIM]
        vt_ref[0, hd] = v.T.astype(jnp.bfloat16)
    zp_ref[0] = z[:, 2 * QK_WIDTH + ATTN_WIDTH:]


def _pre_call(x, g1, win, qg, kg, gmat, cos, sin_signed):
    B, S, _ = x.shape
    T = PRE_ROWS
    return pl.pallas_call(
        _pre_kernel,
        grid=(B, S // T),
        in_specs=[
            pl.BlockSpec((1, T, D_MODEL), lambda b, i: (b, i, 0)),
            _const_spec((1, D_MODEL)),
            _const_spec((D_MODEL, IN_WIDTH)),
            _const_spec((1, HEAD_WIDTH)),
            _const_spec((1, HEAD_WIDTH)),
            _const_spec((HEAD_WIDTH, HEAD_WIDTH)),
            pl.BlockSpec((T, HEAD_WIDTH), lambda b, i: (i, 0)),
            pl.BlockSpec((T, HEAD_WIDTH), lambda b, i: (i, 0)),
        ],
        out_specs=[
            pl.BlockSpec((1, N_HEADS, 2, HEAD_WIDTH, T), lambda b, i: (b, 0, 0, 0, i)),
            pl.BlockSpec((1, T, QK_WIDTH), lambda b, i: (b, i, 0)),
            pl.BlockSpec((1, N_HEADS, V_DIM, T), lambda b, i: (b, 0, 0, i)),
            pl.BlockSpec((1, T, POOL_WIDTH), lambda b, i: (b, i, 0)),
        ],
        out_shape=[
            jax.ShapeDtypeStruct((B, N_HEADS, 2, HEAD_WIDTH, S), jnp.bfloat16),
            jax.ShapeDtypeStruct((B, S, QK_WIDTH), jnp.bfloat16),
            jax.ShapeDtypeStruct((B, N_HEADS, V_DIM, S), jnp.bfloat16),
            jax.ShapeDtypeStruct((B, S, POOL_WIDTH), jnp.float32),
        ],
        compiler_params=pltpu.CompilerParams(
            dimension_semantics=("parallel", "parallel"), vmem_limit_bytes=VMEM_LIMIT),
        name="pre",
    )(x, g1, win, qg, kg, gmat, cos, sin_signed)


def _attn_kernel(lam_ref, bounded_ref, qt_ref, k_ref, vt_ref, g_ref, o_ref,
                 m_ref, l_ref, acc_ref):
    seq_len = k_ref.shape[1]
    q_per_step = o_ref.shape[1] // ATTN_Q
    lam = lam_ref[0]

    def finish(qi, acc, l):
        o = acc[0] / l[0] - lam * (acc[1] / l[1])
        o = o * lax.rsqrt(jnp.mean(o * o, axis=0, keepdims=True) + EPS) * g_ref[...]
        o_ref[0, pl.ds(qi * ATTN_Q, ATTN_Q), :] = (o * (1.0 - LAMBDA_INIT)).T.astype(o_ref.dtype)

    def q_tile(qi, mp):
        return qt_ref[0, 0, mp, :, pl.ds(qi * ATTN_Q, ATTN_Q)]

    @pl.when(bounded_ref[0] != 0)
    def _():
        tk = ATTN_K_UNSHIFTED
        n_kt = seq_len // tk

        def scores(qi, t):
            kt = k_ref[0, pl.ds(t * tk, tk), :]
            return [jnp.dot(kt, q_tile(qi, mp), preferred_element_type=jnp.float32)
                    for mp in range(2)]

        work = [(qi, t) for qi in range(q_per_step) for t in range(n_kt)]
        s = scores(*work[0])
        for n, (qi, t) in enumerate(work):
            s_next = scores(*work[n + 1]) if n + 1 < len(work) else None
            if t == 0:
                l8 = [None, None]
                acc = [None, None]
            vt = vt_ref[0, 0, :, pl.ds(t * tk, tk)]
            for mp in range(2):
                p = jnp.exp(s[mp])
                psum = jnp.sum(p.reshape(tk // 8, 8, ATTN_Q), axis=0)
                pv = jnp.dot(vt, p.astype(jnp.bfloat16), preferred_element_type=jnp.float32)
                l8[mp] = psum if l8[mp] is None else l8[mp] + psum
                acc[mp] = pv if acc[mp] is None else acc[mp] + pv
            if t == n_kt - 1:
                finish(qi, acc, [jnp.sum(x, axis=0, keepdims=True) for x in l8])
            s = s_next

    @pl.when(bounded_ref[0] == 0)
    def _():
        for qi in range(q_per_step):
            m_ref[...] = jnp.full(m_ref.shape, -jnp.inf, jnp.float32)
            l_ref[...] = jnp.zeros(l_ref.shape, jnp.float32)
            acc_ref[...] = jnp.zeros(acc_ref.shape, jnp.float32)

            def body(j, carry):
                start = pl.multiple_of(j * ATTN_K, ATTN_K)
                kt = k_ref[0, pl.ds(start, ATTN_K), :]
                vt = vt_ref[0, 0, :, pl.ds(start, ATTN_K)]
                for mp in range(2):
                    s = jnp.dot(kt, q_tile(qi, mp), preferred_element_type=jnp.float32)
                    m_prev = m_ref[mp]
                    m_new = jnp.maximum(m_prev, jnp.max(s, axis=0, keepdims=True))
                    p = jnp.exp(s - m_new)
                    alpha = jnp.exp(m_prev - m_new)
                    l_ref[mp] = alpha * l_ref[mp] + jnp.sum(p, axis=0, keepdims=True)
                    acc_ref[mp] = alpha * acc_ref[mp] + jnp.dot(
                        vt, p.astype(jnp.bfloat16), preferred_element_type=jnp.float32)
                    m_ref[mp] = m_new
                return carry

            lax.fori_loop(0, seq_len // ATTN_K, body, 0)
            finish(qi, [acc_ref[0], acc_ref[1]], [l_ref[0], l_ref[1]])


def _attn_call(lam, bounded, qt, k, vt, subln_col):
    B, S, _ = k.shape
    q_tiles = min(S // ATTN_Q, max(1, ATTN_UNROLLED_TILES // (S // ATTN_K_UNSHIFTED)))
    q_rows = ATTN_Q * q_tiles
    return pl.pallas_call(
        _attn_kernel,
        grid=(B, N_HEADS, S // q_rows),
        in_specs=[
            pl.BlockSpec(memory_space=pltpu.SMEM),
            pl.BlockSpec(memory_space=pltpu.SMEM),
            pl.BlockSpec((1, 1, 2, HEAD_WIDTH, q_rows), lambda b, h, i: (b, h, 0, 0, i)),
            pl.BlockSpec((1, S, HEAD_WIDTH), lambda b, h, i: (b, 0, h)),
            pl.BlockSpec((1, 1, V_DIM, S), lambda b, h, i: (b, h, 0, 0)),
            _const_spec((V_DIM, 1)),
        ],
        out_specs=pl.BlockSpec((1, q_rows, V_DIM), lambda b, h, i: (b, i, h)),
        out_shape=jax.ShapeDtypeStruct((B, S, ATTN_WIDTH), jnp.bfloat16),
        scratch_shapes=[
            pltpu.VMEM((2, 1, ATTN_Q), jnp.float32),
            pltpu.VMEM((2, 1, ATTN_Q), jnp.float32),
            pltpu.VMEM((2, V_DIM, ATTN_Q), jnp.float32),
        ],
        compiler_params=pltpu.CompilerParams(
            dimension_semantics=("parallel", "parallel", "parallel"),
            vmem_limit_bytes=VMEM_LIMIT),
        name="attn",
    )(lam, bounded, qt, k, vt, subln_col)


def _mix_kernel(x_ref, oa_ref, zp_ref, zprev_ref, znext_ref, wpool_ref, pscale_ref, wout_ref,
                y_ref, ext_ref):
    T = MIX_ROWS
    i = pl.program_id(1)
    n_i = pl.num_programs(1)
    seq_len = n_i * T
    ext_ref[pl.ds(0, HALO), :] = jnp.where(i > 0, zprev_ref[0], 0.0)
    ext_ref[pl.ds(HALO, T), :] = zp_ref[0]
    ext_ref[pl.ds(HALO + T, HALO), :] = jnp.where(i < n_i - 1, znext_ref[0], 0.0)
    pos = i * T + lax.broadcasted_iota(jnp.int32, (T, 1), 0)

    pooled = []
    for g, w in enumerate(POOL_WINDOWS):
        c0 = g * POOL_GROUP_DIM
        tot = jnp.zeros((T, POOL_GROUP_DIM), jnp.float32)
        for off in range(-(w // 2), w // 2):
            tot = tot + ext_ref[pl.ds(HALO + off, T), c0:c0 + POOL_GROUP_DIM]
        lo = jnp.maximum(pos - w // 2, 0)
        hi = jnp.minimum(pos + w // 2 - 1, seq_len - 1)
        cnt = (hi - lo + 1).astype(jnp.float32)
        pg = tot / cnt - ext_ref[pl.ds(HALO, T), c0:c0 + POOL_GROUP_DIM]
        pw = jnp.dot(pg.astype(jnp.bfloat16), wpool_ref[g], preferred_element_type=jnp.float32)
        pooled.append((pw * pscale_ref[:, c0:c0 + POOL_GROUP_DIM]).astype(jnp.bfloat16))
    mixed = jnp.concatenate([oa_ref[0]] + pooled, axis=-1)
    y_ref[0] = x_ref[0] + jnp.dot(mixed, wout_ref[...], preferred_element_type=jnp.float32)


def _mix_call(x, o_attn, zp, wpool, pscale, wout):
    B, S, _ = x.shape
    T = MIX_ROWS
    nb = T // HALO
    last = S // HALO - 1
    return pl.pallas_call(
        _mix_kernel,
        grid=(B, S // T),
        in_specs=[
            pl.BlockSpec((1, T, D_MODEL), lambda b, i: (b, i, 0)),
            pl.BlockSpec((1, T, ATTN_WIDTH), lambda b, i: (b, i, 0)),
            pl.BlockSpec((1, T, POOL_WIDTH), lambda b, i: (b, i, 0)),
            pl.BlockSpec((1, HALO, POOL_WIDTH), lambda b, i: (b, jnp.maximum(i * nb - 1, 0), 0)),
            pl.BlockSpec((1, HALO, POOL_WIDTH), lambda b, i: (b, jnp.minimum((i + 1) * nb, last), 0)),
            _const_spec((len(POOL_WINDOWS), POOL_GROUP_DIM, POOL_GROUP_DIM)),
            _const_spec((1, POOL_WIDTH)),
            _const_spec((ATTN_WIDTH + POOL_WIDTH, D_MODEL)),
        ],
        out_specs=pl.BlockSpec((1, T, D_MODEL), lambda b, i: (b, i, 0)),
        out_shape=jax.ShapeDtypeStruct((B, S, D_MODEL), jnp.float32),
        scratch_shapes=[pltpu.VMEM((T + 2 * HALO, POOL_WIDTH), jnp.float32)],
        compiler_params=pltpu.CompilerParams(
            dimension_semantics=("parallel", "parallel"), vmem_limit_bytes=VMEM_LIMIT),
        name="mix",
    )(x, o_attn, zp, zp, zp, wpool, pscale, wout)


def _ffn_kernel(x_ref, xprev_ref, xnext_ref, g2_ref, wup_ref, cw_ref, cb_ref, wdown_ref,
                y_ref, h_ref, act_ref):
    T = FFN_ROWS
    i = pl.program_id(1)
    n_i = pl.num_programs(1)
    g2 = g2_ref[...]
    h_ref[pl.ds(0, HALO), :] = _rms(jnp.where(i > 0, xprev_ref[0], 0.0), g2).astype(jnp.bfloat16)
    h_ref[pl.ds(HALO, T), :] = _rms(x_ref[0], g2).astype(jnp.bfloat16)
    h_ref[pl.ds(HALO + T, HALO), :] = _rms(
        jnp.where(i < n_i - 1, xnext_ref[0], 0.0), g2).astype(jnp.bfloat16)
    h = h_ref[...]

    def conv(u, c0):
        w = cw_ref[:, c0:c0 + FFN_CHUNK]
        return (u[HALO - 1:HALO - 1 + T] * w[0:1] + u[HALO:HALO + T] * w[1:2]
                + u[HALO + 1:HALO + 1 + T] * w[2:3] + cb_ref[:, c0:c0 + FFN_CHUNK])

    def up(c):
        c0 = c * FFN_CHUNK
        return (jnp.dot(h, wup_ref[:, c0:c0 + FFN_CHUNK], preferred_element_type=jnp.float32),
                jnp.dot(h, wup_ref[:, D_FF + c0:D_FF + c0 + FFN_CHUNK],
                        preferred_element_type=jnp.float32))

    n_chunks = D_FF // FFN_CHUNK
    u = up(0)
    for c in range(n_chunks):
        u_next = up(c + 1) if c + 1 < n_chunks else None
        c0 = c * FFN_CHUNK
        half = 0.5 * conv(u[0], c0)
        val = conv(u[1], D_FF + c0)
        act_ref[:, c0:c0 + FFN_CHUNK] = ((half + half * jnp.tanh(half)) * val).astype(jnp.bfloat16)
        u = u_next
    y_ref[0] = x_ref[0] + jnp.dot(act_ref[...], wdown_ref[...],
                                  preferred_element_type=jnp.float32)


def _ffn_call(x, g2, wup, cw, cb, wdown):
    B, S, _ = x.shape
    T = FFN_ROWS
    nb = T // HALO
    last = S // HALO - 1
    single = pl.Buffered(1)
    return pl.pallas_call(
        _ffn_kernel,
        grid=(B, S // T),
        in_specs=[
            pl.BlockSpec((1, T, D_MODEL), lambda b, i: (b, i, 0)),
            pl.BlockSpec((1, HALO, D_MODEL), lambda b, i: (b, jnp.maximum(i * nb - 1, 0), 0)),
            pl.BlockSpec((1, HALO, D_MODEL), lambda b, i: (b, jnp.minimum((i + 1) * nb, last), 0)),
            _const_spec((1, D_MODEL)),
            pl.BlockSpec((D_MODEL, 2 * D_FF), lambda b, i: (0, 0), pipeline_mode=single),
            _const_spec((3, 2 * D_FF)),
            _const_spec((1, 2 * D_FF)),
            pl.BlockSpec((D_FF, D_MODEL), lambda b, i: (0, 0), pipeline_mode=single),
        ],
        out_specs=pl.BlockSpec((1, T, D_MODEL), lambda b, i: (b, i, 0)),
        out_shape=jax.ShapeDtypeStruct((B, S, D_MODEL), jnp.float32),
        scratch_shapes=[
            pltpu.VMEM((T + 2 * HALO, D_MODEL), jnp.bfloat16),
            pltpu.VMEM((T, D_FF), jnp.bfloat16),
        ],
        compiler_params=pltpu.CompilerParams(
            dimension_semantics=("parallel", "parallel"), vmem_limit_bytes=VMEM_LIMIT),
        name="ffn",
    )(x, x, x, g2, wup, cw, cb, wdown)


def _rope_tables(seq_len):
    inv = ROPE_THETA ** (-jnp.arange(0, HEAD_DIM, 2, dtype=jnp.float32) / HEAD_DIM)
    ang = jnp.arange(seq_len, dtype=jnp.float32)[:, None] * inv[None, :]
    ang = jnp.concatenate([ang, ang], axis=-1)
    cos = jnp.cos(ang)
    sin = jnp.sin(ang)
    half = HEAD_DIM // 2
    sin_signed = jnp.concatenate([-sin[:, :half], sin[:, half:]], axis=-1)
    return jnp.tile(cos, (1, 2)), jnp.tile(sin_signed, (1, 2))


def _layer(x, p):
    B, S, _ = x.shape
    cos, sin_signed = _rope_tables(S)
    qt, k, vt, zp = _pre_call(x, p["g1"], p["win"], p["qg"], p["kg"], p["gmat"], cos, sin_signed)
    o_attn = _attn_call(p["lam"], p["bounded"], qt, k, vt, p["subln"])
    x1 = _mix_call(x, o_attn, zp, p["wpool"], p["pscale"], p["wout"])
    return _ffn_call(x1, p["g2"], p["wup"], p["cw"], p["cb"], p["wdown"])


def kernel(x_prompt, x_sample, norm1_g, w_in, q_norm_g, k_norm_g, lambda_q1, lambda_k1,
           lambda_q2, lambda_k2, subln_g, w_pool, pool_scale, w_out, norm2_g, w_up, conv_w,
           conv_b, w_down):
    f32 = jnp.float32
    bf16 = jnp.bfloat16
    group = jnp.arange(HEAD_WIDTH) // HEAD_DIM
    lam = (jnp.exp(jnp.sum(lambda_q1[0].astype(f32) * lambda_k1[0].astype(f32)))
           - jnp.exp(jnp.sum(lambda_q2[0].astype(f32) * lambda_k2[0].astype(f32)))
           + LAMBDA_INIT)
    logit_bound = (1.01 * math.sqrt(HEAD_DIM) * jnp.max(jnp.abs(q_norm_g[0].astype(f32)))
                   * jnp.max(jnp.abs(k_norm_g[0].astype(f32))))
    p = {
        "g1": norm1_g[0].reshape(1, D_MODEL),
        "win": w_in[0].astype(bf16),
        "qg": jnp.tile(q_norm_g[0], 2).reshape(1, HEAD_WIDTH),
        "kg": jnp.tile(k_norm_g[0], 2).reshape(1, HEAD_WIDTH),
        "gmat": (group[:, None] == group[None, :]).astype(bf16),
        "lam": lam.reshape(1),
        "bounded": (logit_bound < MAX_UNSHIFTED_LOGIT).astype(jnp.int32).reshape(1),
        "subln": subln_g[0].reshape(V_DIM, 1),
        "wpool": w_pool[0].astype(bf16),
        "pscale": pool_scale[0].reshape(1, POOL_WIDTH),
        "wout": w_out[0].astype(bf16),
        "g2": norm2_g[0].reshape(1, D_MODEL),
        "wup": w_up[0].astype(bf16),
        "cw": conv_w[0],
        "cb": conv_b[0].reshape(1, 2 * D_FF),
        "wdown": w_down[0].astype(bf16),
    }
    return (_layer(x_prompt, p), _layer(x_sample, p))
```

```python
import functools
import math

import jax
import jax.numpy as jnp
from jax import lax
from jax.experimental import pallas as pl
from jax.experimental.pallas import tpu as pltpu

D_MODEL = 1024
N_HEADS = 4
HEAD_DIM = 64
V_DIM = 2 * HEAD_DIM
HEAD_WIDTH = 2 * HEAD_DIM
ATTN_WIDTH = N_HEADS * V_DIM
QK_WIDTH = N_HEADS * 2 * HEAD_DIM
POOL_WINDOWS = (2, 4, 8, 16)
POOL_GROUP_DIM = 128
POOL_WIDTH = len(POOL_WINDOWS) * POOL_GROUP_DIM
IN_WIDTH = 2 * QK_WIDTH + ATTN_WIDTH + POOL_WIDTH
D_FF = 2816
ROPE_THETA = 10000.0
EPS = 1e-6
LAMBDA_INIT = 0.8 - 0.6 * math.exp(-0.3 * 0)
MAX_UNSHIFTED_LOGIT = 30.0

HALO = 8
VMEM_LIMIT = 56 * 1024 * 1024

PRE_ROWS = 512
PRE_CHUNK = 256
ATTN_Q = 256
ATTN_K = 512
ATTN_UNROLLED_TILES = 64
ATTN_K_UNSHIFTED = 256
MIX_ROWS = 512
FFN_ROWS = 512
FFN_CHUNK = 256


def _const_spec(shape):
    return pl.BlockSpec(shape, lambda *_: (0,) * len(shape))


def _rms(x, gain):
    return x * lax.rsqrt(jnp.mean(x * x, axis=-1, keepdims=True) + EPS) * gain


def _group_mean_square(x, gmat):
    sq = x * x
    hi = sq.astype(jnp.bfloat16)
    lo = (sq - hi.astype(jnp.float32)).astype(jnp.bfloat16)
    tot = (jnp.dot(hi, gmat, preferred_element_type=jnp.float32)
           + jnp.dot(lo, gmat, preferred_element_type=jnp.float32))
    return tot * (1.0 / HEAD_DIM)


def _norm_rope(x, gain, gmat, cos, sin_signed, first_half):
    xn = x * lax.rsqrt(_group_mean_square(x, gmat) + EPS) * gain
    rot = jnp.where(first_half,
                    pltpu.roll(xn, HEAD_WIDTH - HEAD_DIM // 2, axis=1),
                    pltpu.roll(xn, HEAD_DIM // 2, axis=1))
    return xn * cos + rot * sin_signed


def _pre_kernel(x_ref, g1_ref, win_ref, qg_ref, kg_ref, gmat_ref, cos_ref, sin_ref,
                qt_ref, k_ref, vt_ref, zp_ref):
    x = x_ref[0]
    h = _rms(x, g1_ref[...]).astype(jnp.bfloat16)
    gmat = gmat_ref[...]
    cos = cos_ref[...]
    sin_signed = sin_ref[...]
    lane = lax.broadcasted_iota(jnp.int32, (1, HEAD_WIDTH), 1)
    first_half = (lane % HEAD_DIM) < (HEAD_DIM // 2)
    row = lax.broadcasted_iota(jnp.int32, (HEAD_WIDTH, 1), 0)
    scale = math.log2(math.e) / math.sqrt(HEAD_DIM)

    def proj(c):
        return jnp.dot(h, win_ref[:, c * PRE_CHUNK:(c + 1) * PRE_CHUNK],
                       preferred_element_type=jnp.float32)

    n_chunks = IN_WIDTH // PRE_CHUNK
    z = proj(0)
    for c in range(n_chunks):
        z_next = proj(c + 1) if c + 1 < n_chunks else None
        if c < N_HEADS:
            q = _norm_rope(z[:, :HEAD_WIDTH], qg_ref[...], gmat, cos, sin_signed, first_half)
            qt = (q * scale).T
            qt_ref[0, c, 0] = jnp.where(row < HEAD_DIM, qt, 0.0).astype(jnp.bfloat16)
            qt_ref[0, c, 1] = jnp.where(row >= HEAD_DIM, qt, 0.0).astype(jnp.bfloat16)
            k = _norm_rope(z[:, HEAD_WIDTH:], kg_ref[...], gmat, cos, sin_signed, first_half)
            k_ref[0, :, c * HEAD_WIDTH:(c + 1) * HEAD_WIDTH] = k.astype(jnp.bfloat16)
        elif c < N_HEADS + N_HEADS // 2:
            for j in range(2):
                hd = 2 * (c - N_HEADS) + j
                vt_ref[0, hd] = z[:, j * V_DIM:(j + 1) * V_DIM].T.astype(jnp.bfloat16)
        else:
            c0 = (c - N_HEADS - N_HEADS // 2) * PRE_CHUNK
            zp_ref[0, :, c0:c0 + PRE_CHUNK] = z
        z = z_next


def _pre_call(x, g1, win, qg, kg, gmat, cos, sin_signed):
    B, S, _ = x.shape
    T = PRE_ROWS
    return pl.pallas_call(
        _pre_kernel,
        grid=(B, S // T),
        in_specs=[
            pl.BlockSpec((1, T, D_MODEL), lambda b, i: (b, i, 0)),
            _const_spec((1, D_MODEL)),
            _const_spec((D_MODEL, IN_WIDTH)),
            _const_spec((1, HEAD_WIDTH)),
            _const_spec((1, HEAD_WIDTH)),
            _const_spec((HEAD_WIDTH, HEAD_WIDTH)),
            pl.BlockSpec((T, HEAD_WIDTH), lambda b, i: (i, 0)),
            pl.BlockSpec((T, HEAD_WIDTH), lambda b, i: (i, 0)),
        ],
        out_specs=[
            pl.BlockSpec((1, N_HEADS, 2, HEAD_WIDTH, T), lambda b, i: (b, 0, 0, 0, i)),
            pl.BlockSpec((1, T, QK_WIDTH), lambda b, i: (b, i, 0)),
            pl.BlockSpec((1, N_HEADS, V_DIM, T), lambda b, i: (b, 0, 0, i)),
            pl.BlockSpec((1, T, POOL_WIDTH), lambda b, i: (b, i, 0)),
        ],
        out_shape=[
            jax.ShapeDtypeStruct((B, N_HEADS, 2, HEAD_WIDTH, S), jnp.bfloat16),
            jax.ShapeDtypeStruct((B, S, QK_WIDTH), jnp.bfloat16),
            jax.ShapeDtypeStruct((B, N_HEADS, V_DIM, S), jnp.bfloat16),
            jax.ShapeDtypeStruct((B, S, POOL_WIDTH), jnp.float32),
        ],
        compiler_params=pltpu.CompilerParams(
            dimension_semantics=("parallel", "parallel"), vmem_limit_bytes=VMEM_LIMIT),
        name="pre",
    )(x, g1, win, qg, kg, gmat, cos, sin_signed)


def _attn_kernel(lam_ref, bounded_ref, qt_ref, k_ref, vt_ref, g_ref, o_ref,
                 m_ref, l_ref, acc_ref):
    seq_len = k_ref.shape[1]
    q_per_step = o_ref.shape[1] // ATTN_Q
    lam = lam_ref[0]

    def finish(qi, acc, l):
        o = acc[0] / l[0] - lam * (acc[1] / l[1])
        o = o * lax.rsqrt(jnp.mean(o * o, axis=0, keepdims=True) + EPS) * g_ref[...]
        o_ref[0, pl.ds(qi * ATTN_Q, ATTN_Q), :] = (o * (1.0 - LAMBDA_INIT)).T.astype(o_ref.dtype)

    def q_tile(qi, mp):
        return qt_ref[0, 0, mp, :, pl.ds(qi * ATTN_Q, ATTN_Q)]

    @pl.when(bounded_ref[0] != 0)
    def _():
        tk = ATTN_K_UNSHIFTED
        n_kt = seq_len // tk

        def scores(qi, t):
            kt = k_ref[0, pl.ds(t * tk, tk), :]
            return [jnp.dot(kt, q_tile(qi, mp), preferred_element_type=jnp.float32)
                    for mp in range(2)]

        work = [(qi, t) for qi in range(q_per_step) for t in range(n_kt)]
        s = scores(*work[0])
        for n, (qi, t) in enumerate(work):
            s_next = scores(*work[n + 1]) if n + 1 < len(work) else None
            if t == 0:
                l8 = [None, None]
                acc = [None, None]
            vt = vt_ref[0, 0, :, pl.ds(t * tk, tk)]
            for mp in range(2):
                p = jnp.exp2(s[mp])
                psum = jnp.sum(p.reshape(tk // 8, 8, ATTN_Q), axis=0)
                pv = jnp.dot(vt, p.astype(jnp.bfloat16), preferred_element_type=jnp.float32)
                l8[mp] = psum if l8[mp] is None else l8[mp] + psum
                acc[mp] = pv if acc[mp] is None else acc[mp] + pv
            if t == n_kt - 1:
                finish(qi, acc, [jnp.sum(x, axis=0, keepdims=True) for x in l8])
            s = s_next

    @pl.when(bounded_ref[0] == 0)
    def _():
        for qi in range(q_per_step):
            m_ref[...] = jnp.full(m_ref.shape, -jnp.inf, jnp.float32)
            l_ref[...] = jnp.zeros(l_ref.shape, jnp.float32)
            acc_ref[...] = jnp.zeros(acc_ref.shape, jnp.float32)

            def body(j, carry):
                start = pl.multiple_of(j * ATTN_K, ATTN_K)
                kt = k_ref[0, pl.ds(start, ATTN_K), :]
                vt = vt_ref[0, 0, :, pl.ds(start, ATTN_K)]
                for mp in range(2):
                    s = jnp.dot(kt, q_tile(qi, mp), preferred_element_type=jnp.float32)
                    m_prev = m_ref[mp]
                    m_new = jnp.maximum(m_prev, jnp.max(s, axis=0, keepdims=True))
                    p = jnp.exp2(s - m_new)
                    alpha = jnp.exp2(m_prev - m_new)
                    l_ref[mp] = alpha * l_ref[mp] + jnp.sum(p, axis=0, keepdims=True)
                    acc_ref[mp] = alpha * acc_ref[mp] + jnp.dot(
                        vt, p.astype(jnp.bfloat16), preferred_element_type=jnp.float32)
                    m_ref[mp] = m_new
                return carry

            lax.fori_loop(0, seq_len // ATTN_K, body, 0)
            finish(qi, [acc_ref[0], acc_ref[1]], [l_ref[0], l_ref[1]])


def _attn_call(lam, bounded, qt, k, vt, subln_col):
    B, S, _ = k.shape
    q_tiles = min(S // ATTN_Q, max(1, ATTN_UNROLLED_TILES // (S // ATTN_K_UNSHIFTED)))
    q_rows = ATTN_Q * q_tiles
    return pl.pallas_call(
        _attn_kernel,
        grid=(B, N_HEADS, S // q_rows),
        in_specs=[
            pl.BlockSpec(memory_space=pltpu.SMEM),
            pl.BlockSpec(memory_space=pltpu.SMEM),
            pl.BlockSpec((1, 1, 2, HEAD_WIDTH, q_rows), lambda b, h, i: (b, h, 0, 0, i)),
            pl.BlockSpec((1, S, HEAD_WIDTH), lambda b, h, i: (b, 0, h)),
            pl.BlockSpec((1, 1, V_DIM, S), lambda b, h, i: (b, h, 0, 0)),
            _const_spec((V_DIM, 1)),
        ],
        out_specs=pl.BlockSpec((1, q_rows, V_DIM), lambda b, h, i: (b, i, h)),
        out_shape=jax.ShapeDtypeStruct((B, S, ATTN_WIDTH), jnp.bfloat16),
        scratch_shapes=[
            pltpu.VMEM((2, 1, ATTN_Q), jnp.float32),
            pltpu.VMEM((2, 1, ATTN_Q), jnp.float32),
            pltpu.VMEM((2, V_DIM, ATTN_Q), jnp.float32),
        ],
        compiler_params=pltpu.CompilerParams(
            dimension_semantics=("parallel", "parallel", "parallel"),
            vmem_limit_bytes=VMEM_LIMIT),
        name="attn",
    )(lam, bounded, qt, k, vt, subln_col)


def _mix_kernel(x_ref, oa_ref, zp_ref, zprev_ref, znext_ref, wpool_ref, pscale_ref, wout_ref,
                y_ref, ext_ref):
    T = MIX_ROWS
    i = pl.program_id(1)
    n_i = pl.num_programs(1)
    seq_len = n_i * T
    ext_ref[pl.ds(0, HALO), :] = jnp.where(i > 0, zprev_ref[0], 0.0)
    ext_ref[pl.ds(HALO, T), :] = zp_ref[0]
    ext_ref[pl.ds(HALO + T, HALO), :] = jnp.where(i < n_i - 1, znext_ref[0], 0.0)
    pos = i * T + lax.broadcasted_iota(jnp.int32, (T, 1), 0)

    y = x_ref[0] + jnp.dot(oa_ref[0], wout_ref[:ATTN_WIDTH, :], preferred_element_type=jnp.float32)

    pooled = []
    for g, w in enumerate(POOL_WINDOWS):
        c0 = g * POOL_GROUP_DIM
        tot = jnp.zeros((T, POOL_GROUP_DIM), jnp.float32)
        for off in range(-(w // 2), w // 2):
            tot = tot + ext_ref[pl.ds(HALO + off, T), c0:c0 + POOL_GROUP_DIM]
        lo = jnp.maximum(pos - w // 2, 0)
        hi = jnp.minimum(pos + w // 2 - 1, seq_len - 1)
        cnt = (hi - lo + 1).astype(jnp.float32)
        pg = tot / cnt - ext_ref[pl.ds(HALO, T), c0:c0 + POOL_GROUP_DIM]
        pw = jnp.dot(pg.astype(jnp.bfloat16), wpool_ref[g], preferred_element_type=jnp.float32)
        pooled.append((pw * pscale_ref[:, c0:c0 + POOL_GROUP_DIM]).astype(jnp.bfloat16))
    o_pool = jnp.concatenate(pooled, axis=-1)
    y_ref[0] = y + jnp.dot(o_pool, wout_ref[ATTN_WIDTH:, :], preferred_element_type=jnp.float32)


def _mix_call(x, o_attn, zp, wpool, pscale, wout):
    B, S, _ = x.shape
    T = MIX_ROWS
    nb = T // HALO
    last = S // HALO - 1
    return pl.pallas_call(
        _mix_kernel,
        grid=(B, S // T),
        in_specs=[
            pl.BlockSpec((1, T, D_MODEL), lambda b, i: (b, i, 0)),
            pl.BlockSpec((1, T, ATTN_WIDTH), lambda b, i: (b, i, 0)),
            pl.BlockSpec((1, T, POOL_WIDTH), lambda b, i: (b, i, 0)),
            pl.BlockSpec((1, HALO, POOL_WIDTH), lambda b, i: (b, jnp.maximum(i * nb - 1, 0), 0)),
            pl.BlockSpec((1, HALO, POOL_WIDTH), lambda b, i: (b, jnp.minimum((i + 1) * nb, last), 0)),
            _const_spec((len(POOL_WINDOWS), POOL_GROUP_DIM, POOL_GROUP_DIM)),
            _const_spec((1, POOL_WIDTH)),
            _const_spec((ATTN_WIDTH + POOL_WIDTH, D_MODEL)),
        ],
        out_specs=pl.BlockSpec((1, T, D_MODEL), lambda b, i: (b, i, 0)),
        out_shape=jax.ShapeDtypeStruct((B, S, D_MODEL), jnp.float32),
        scratch_shapes=[pltpu.VMEM((T + 2 * HALO, POOL_WIDTH), jnp.float32)],
        compiler_params=pltpu.CompilerParams(
            dimension_semantics=("parallel", "parallel"), vmem_limit_bytes=VMEM_LIMIT),
        name="mix",
    )(x, o_attn, zp, zp, zp, wpool, pscale, wout)


def _ffn_kernel(x_ref, xprev_ref, xnext_ref, g2_ref, wup_ref, cw_ref, cb_ref, wdown_ref,
                y_ref, h_ref, act_ref):
    T = FFN_ROWS
    i = pl.program_id(1)
    n_i = pl.num_programs(1)
    g2 = g2_ref[...]
    h_ref[pl.ds(0, HALO), :] = _rms(jnp.where(i > 0, xprev_ref[0], 0.0), g2).astype(jnp.bfloat16)
    h_ref[pl.ds(HALO, T), :] = _rms(x_ref[0], g2).astype(jnp.bfloat16)
    h_ref[pl.ds(HALO + T, HALO), :] = _rms(
        jnp.where(i < n_i - 1, xnext_ref[0], 0.0), g2).astype(jnp.bfloat16)
    h = h_ref[...]

    def conv(u, c0):
        w = cw_ref[:, c0:c0 + FFN_CHUNK]
        return (u[HALO - 1:HALO - 1 + T] * w[0:1] + u[HALO:HALO + T] * w[1:2]
                + u[HALO + 1:HALO + 1 + T] * w[2:3] + cb_ref[:, c0:c0 + FFN_CHUNK])

    def up(c):
        c0 = c * FFN_CHUNK
        return (jnp.dot(h, wup_ref[:, c0:c0 + FFN_CHUNK], preferred_element_type=jnp.float32),
                jnp.dot(h, wup_ref[:, D_FF + c0:D_FF + c0 + FFN_CHUNK],
                        preferred_element_type=jnp.float32))

    n_chunks = D_FF // FFN_CHUNK
    u = up(0)
    for c in range(n_chunks):
        u_next = up(c + 1) if c + 1 < n_chunks else None
        c0 = c * FFN_CHUNK
        half = 0.5 * conv(u[0], c0)
        val = conv(u[1], D_FF + c0)
        act_ref[:, c0:c0 + FFN_CHUNK] = ((half + half * jnp.tanh(half)) * val).astype(jnp.bfloat16)
        u = u_next
    y_ref[0] = x_ref[0] + jnp.dot(act_ref[...], wdown_ref[...],
                                  preferred_element_type=jnp.float32)


def _ffn_call(x, g2, wup, cw, cb, wdown):
    B, S, _ = x.shape
    T = FFN_ROWS
    nb = T // HALO
    last = S // HALO - 1
    single = pl.Buffered(1)
    return pl.pallas_call(
        _ffn_kernel,
        grid=(B, S // T),
        in_specs=[
            pl.BlockSpec((1, T, D_MODEL), lambda b, i: (b, i, 0)),
            pl.BlockSpec((1, HALO, D_MODEL), lambda b, i: (b, jnp.maximum(i * nb - 1, 0), 0)),
            pl.BlockSpec((1, HALO, D_MODEL), lambda b, i: (b, jnp.minimum((i + 1) * nb, last), 0)),
            _const_spec((1, D_MODEL)),
            pl.BlockSpec((D_MODEL, 2 * D_FF), lambda b, i: (0, 0), pipeline_mode=single),
            _const_spec((3, 2 * D_FF)),
            _const_spec((1, 2 * D_FF)),
            pl.BlockSpec((D_FF, D_MODEL), lambda b, i: (0, 0), pipeline_mode=single),
        ],
        out_specs=pl.BlockSpec((1, T, D_MODEL), lambda b, i: (b, i, 0)),
        out_shape=jax.ShapeDtypeStruct((B, S, D_MODEL), jnp.float32),
        scratch_shapes=[
            pltpu.VMEM((T + 2 * HALO, D_MODEL), jnp.bfloat16),
            pltpu.VMEM((T, D_FF), jnp.bfloat16),
        ],
        compiler_params=pltpu.CompilerParams(
            dimension_semantics=("parallel", "parallel"), vmem_limit_bytes=VMEM_LIMIT),
        name="ffn",
    )(x, x, x, g2, wup, cw, cb, wdown)


def _rope_tables(seq_len):
    inv = ROPE_THETA ** (-jnp.arange(0, HEAD_DIM, 2, dtype=jnp.float32) / HEAD_DIM)
    ang = jnp.arange(seq_len, dtype=jnp.float32)[:, None] * inv[None, :]
    ang = jnp.concatenate([ang, ang], axis=-1)
    cos = jnp.cos(ang)
    sin = jnp.sin(ang)
    half = HEAD_DIM // 2
    sin_signed = jnp.concatenate([-sin[:, :half], sin[:, half:]], axis=-1)
    return jnp.tile(cos, (1, 2)), jnp.tile(sin_signed, (1, 2))


def _regroup_w_in(w):
    cols = []
    for hd in range(N_HEADS):
        lo = hd * HEAD_WIDTH
        cols += [w[:, lo:lo + HEAD_WIDTH], w[:, QK_WIDTH + lo:QK_WIDTH + lo + HEAD_WIDTH]]
    cols.append(w[:, 2 * QK_WIDTH:])
    return jnp.concatenate(cols, axis=1)


def _layer(x, p):
    B, S, _ = x.shape
    cos, sin_signed = _rope_tables(S)
    qt, k, vt, zp = _pre_call(x, p["g1"], p["win"], p["qg"], p["kg"], p["gmat"], cos, sin_signed)
    o_attn = _attn_call(p["lam"], p["bounded"], qt, k, vt, p["subln"])
    x1 = _mix_call(x, o_attn, zp, p["wpool"], p["pscale"], p["wout"])
    return _ffn_call(x1, p["g2"], p["wup"], p["cw"], p["cb"], p["wdown"])


def kernel(x_prompt, x_sample, norm1_g, w_in, q_norm_g, k_norm_g, lambda_q1, lambda_k1,
           lambda_q2, lambda_k2, subln_g, w_pool, pool_scale, w_out, norm2_g, w_up, conv_w,
           conv_b, w_down):
    f32 = jnp.float32
    bf16 = jnp.bfloat16
    group = jnp.arange(HEAD_WIDTH) // HEAD_DIM
    lam = (jnp.exp(jnp.sum(lambda_q1[0].astype(f32) * lambda_k1[0].astype(f32)))
           - jnp.exp(jnp.sum(lambda_q2[0].astype(f32) * lambda_k2[0].astype(f32)))
           + LAMBDA_INIT)
    logit_bound = (1.01 * math.sqrt(HEAD_DIM) * jnp.max(jnp.abs(q_norm_g[0].astype(f32)))
                   * jnp.max(jnp.abs(k_norm_g[0].astype(f32))))
    p = {
        "g1": norm1_g[0].reshape(1, D_MODEL),
        "win": _regroup_w_in(w_in[0]).astype(bf16),
        "qg": jnp.tile(q_norm_g[0], 2).reshape(1, HEAD_WIDTH),
        "kg": jnp.tile(k_norm_g[0], 2).reshape(1, HEAD_WIDTH),
        "gmat": (group[:, None] == group[None, :]).astype(bf16),
        "lam": lam.reshape(1),
        "bounded": (logit_bound < MAX_UNSHIFTED_LOGIT).astype(jnp.int32).reshape(1),
        "subln": subln_g[0].reshape(V_DIM, 1),
        "wpool": w_pool[0].astype(bf16),
        "pscale": pool_scale[0].reshape(1, POOL_WIDTH),
        "wout": w_out[0].astype(bf16),
        "g2": norm2_g[0].reshape(1, D_MODEL),
        "wup": w_up[0].astype(bf16),
        "cw": conv_w[0],
        "cb": conv_b[0].reshape(1, 2 * D_FF),
        "wdown": w_down[0].astype(bf16),
    }
    return (_layer(x_prompt, p), _layer(x_sample, p))
```

```python
import functools
import math

import jax
import jax.numpy as jnp
from jax import lax
from jax.experimental import pallas as pl
from jax.experimental.pallas import tpu as pltpu

D_MODEL = 1024
N_HEADS = 4
HEAD_DIM = 64
V_DIM = 2 * HEAD_DIM
HEAD_WIDTH = 2 * HEAD_DIM
ATTN_WIDTH = N_HEADS * V_DIM
QK_WIDTH = N_HEADS * 2 * HEAD_DIM
POOL_WINDOWS = (2, 4, 8, 16)
POOL_GROUP_DIM = 128
POOL_WIDTH = len(POOL_WINDOWS) * POOL_GROUP_DIM
IN_WIDTH = 2 * QK_WIDTH + ATTN_WIDTH + POOL_WIDTH
D_FF = 2816
ROPE_THETA = 10000.0
EPS = 1e-6
LAMBDA_INIT = 0.8 - 0.6 * math.exp(-0.3 * 0)
MAX_UNSHIFTED_LOGIT = 30.0

HALO = 8
VMEM_LIMIT = 56 * 1024 * 1024

PRE_ROWS = 512
PRE_CHUNK = 256
ATTN_Q = 256
ATTN_K = 512
ATTN_UNROLLED_TILES = 128
ATTN_K_UNSHIFTED = 256
MIX_ROWS = 512
FFN_ROWS = 1024
FFN_CHUNK = 256


def _const_spec(shape):
    return pl.BlockSpec(shape, lambda *_: (0,) * len(shape))


def _rms(x, gain):
    return x * lax.rsqrt(jnp.mean(x * x, axis=-1, keepdims=True) + EPS) * gain


def _group_mean_square(x, gmat):
    sq = x * x
    hi = sq.astype(jnp.bfloat16)
    lo = (sq - hi.astype(jnp.float32)).astype(jnp.bfloat16)
    tot = (jnp.dot(hi, gmat, preferred_element_type=jnp.float32)
           + jnp.dot(lo, gmat, preferred_element_type=jnp.float32))
    return tot * (1.0 / HEAD_DIM)


def _norm_rope(x, gain, gmat, cos, sin_signed, first_half):
    xn = x * lax.rsqrt(_group_mean_square(x, gmat) + EPS) * gain
    rot = jnp.where(first_half,
                    pltpu.roll(xn, HEAD_WIDTH - HEAD_DIM // 2, axis=1),
                    pltpu.roll(xn, HEAD_DIM // 2, axis=1))
    return xn * cos + rot * sin_signed


def _pre_kernel(x_ref, g1_ref, win_ref, qg_ref, kg_ref, gmat_ref, cos_ref, sin_ref,
                qt_ref, k_ref, vt_ref, zp_ref):
    x = x_ref[0]
    h = _rms(x, g1_ref[...]).astype(jnp.bfloat16)
    gmat = gmat_ref[...]
    cos = cos_ref[...]
    sin_signed = sin_ref[...]
    lane = lax.broadcasted_iota(jnp.int32, (1, HEAD_WIDTH), 1)
    first_half = (lane % HEAD_DIM) < (HEAD_DIM // 2)
    row = lax.broadcasted_iota(jnp.int32, (HEAD_WIDTH, 1), 0)
    scale = math.log2(math.e) / math.sqrt(HEAD_DIM)

    def proj(c):
        return jnp.dot(h, win_ref[:, c * PRE_CHUNK:(c + 1) * PRE_CHUNK],
                       preferred_element_type=jnp.float32)

    n_chunks = IN_WIDTH // PRE_CHUNK
    z = proj(0)
    for c in range(n_chunks):
        z_next = proj(c + 1) if c + 1 < n_chunks else None
        if c < N_HEADS:
            q = _norm_rope(z[:, :HEAD_WIDTH], qg_ref[...], gmat, cos, sin_signed, first_half)
            qt = (q * scale).T
            qt_ref[0, c, 0] = jnp.where(row < HEAD_DIM, qt, 0.0).astype(jnp.bfloat16)
            qt_ref[0, c, 1] = jnp.where(row >= HEAD_DIM, qt, 0.0).astype(jnp.bfloat16)
            k = _norm_rope(z[:, HEAD_WIDTH:], kg_ref[...], gmat, cos, sin_signed, first_half)
            k_ref[0, :, c * HEAD_WIDTH:(c + 1) * HEAD_WIDTH] = k.astype(jnp.bfloat16)
        elif c < N_HEADS + N_HEADS // 2:
            for j in range(2):
                hd = 2 * (c - N_HEADS) + j
                vt_ref[0, hd] = z[:, j * V_DIM:(j + 1) * V_DIM].T.astype(jnp.bfloat16)
        else:
            c0 = (c - N_HEADS - N_HEADS // 2) * PRE_CHUNK
            zp_ref[0, :, c0:c0 + PRE_CHUNK] = z
        z = z_next


def _pre_call(x, g1, win, qg, kg, gmat, cos, sin_signed):
    B, S, _ = x.shape
    T = PRE_ROWS
    return pl.pallas_call(
        _pre_kernel,
        grid=(B, S // T),
        in_specs=[
            pl.BlockSpec((1, T, D_MODEL), lambda b, i: (b, i, 0)),
            _const_spec((1, D_MODEL)),
            _const_spec((D_MODEL, IN_WIDTH)),
            _const_spec((1, HEAD_WIDTH)),
            _const_spec((1, HEAD_WIDTH)),
            _const_spec((HEAD_WIDTH, HEAD_WIDTH)),
            pl.BlockSpec((T, HEAD_WIDTH), lambda b, i: (i, 0)),
            pl.BlockSpec((T, HEAD_WIDTH), lambda b, i: (i, 0)),
        ],
        out_specs=[
            pl.BlockSpec((1, N_HEADS, 2, HEAD_WIDTH, T), lambda b, i: (b, 0, 0, 0, i)),
            pl.BlockSpec((1, T, QK_WIDTH), lambda b, i: (b, i, 0)),
            pl.BlockSpec((1, N_HEADS, V_DIM, T), lambda b, i: (b, 0, 0, i)),
            pl.BlockSpec((1, T, POOL_WIDTH), lambda b, i: (b, i, 0)),
        ],
        out_shape=[
            jax.ShapeDtypeStruct((B, N_HEADS, 2, HEAD_WIDTH, S), jnp.bfloat16),
            jax.ShapeDtypeStruct((B, S, QK_WIDTH), jnp.bfloat16),
            jax.ShapeDtypeStruct((B, N_HEADS, V_DIM, S), jnp.bfloat16),
            jax.ShapeDtypeStruct((B, S, POOL_WIDTH), jnp.float32),
        ],
        compiler_params=pltpu.CompilerParams(
            dimension_semantics=("parallel", "parallel"), vmem_limit_bytes=VMEM_LIMIT),
        name="pre",
    )(x, g1, win, qg, kg, gmat, cos, sin_signed)


def _attn_kernel(lam_ref, bounded_ref, qt_ref, k_ref, vt_ref, g_ref, o_ref,
                 m_ref, l_ref, acc_ref):
    seq_len = k_ref.shape[1]
    q_per_step = o_ref.shape[1] // ATTN_Q
    lam = lam_ref[0]

    def finish(qi, acc, l):
        o = acc[0] / l[0] - lam * (acc[1] / l[1])
        o = o * lax.rsqrt(jnp.mean(o * o, axis=0, keepdims=True) + EPS) * g_ref[...]
        o_ref[0, pl.ds(qi * ATTN_Q, ATTN_Q), :] = (o * (1.0 - LAMBDA_INIT)).T.astype(o_ref.dtype)

    def q_tile(qi, mp):
        return qt_ref[0, 0, mp, :, pl.ds(qi * ATTN_Q, ATTN_Q)]

    @pl.when(bounded_ref[0] != 0)
    def _():
        tk = ATTN_K_UNSHIFTED
        bounds = [list(range(0, seq_len + 1, tk)),
                  [0] + list(range(tk // 2, seq_len, tk)) + [seq_len]]
        work = [[(qi, lo, hi - lo) for qi in range(q_per_step)
                 for lo, hi in zip(bounds[mp][:-1], bounds[mp][1:])] for mp in range(2)]

        def scores(mp, qi, lo, size):
            return jnp.dot(k_ref[0, pl.ds(lo, size), :], q_tile(qi, mp),
                           preferred_element_type=jnp.float32)

        s = [scores(mp, *work[mp][0]) for mp in range(2)]
        l8 = [None, None]
        acc = [None, None]
        done = {}
        for n in range(max(len(w) for w in work)):
            for mp in range(2):
                if n >= len(work[mp]):
                    continue
                qi, lo, size = work[mp][n]
                s_next = scores(mp, *work[mp][n + 1]) if n + 1 < len(work[mp]) else None
                p = jnp.exp2(s[mp])
                psum = jnp.sum(p.reshape(size // 8, 8, ATTN_Q), axis=0)
                pv = jnp.dot(vt_ref[0, 0, :, pl.ds(lo, size)], p.astype(jnp.bfloat16),
                             preferred_element_type=jnp.float32)
                l8[mp] = psum if lo == 0 else l8[mp] + psum
                acc[mp] = pv if lo == 0 else acc[mp] + pv
                if lo + size == seq_len:
                    done.setdefault(qi, {})[mp] = (acc[mp], jnp.sum(l8[mp], axis=0, keepdims=True))
                    if len(done[qi]) == 2:
                        finish(qi, [done[qi][m][0] for m in range(2)],
                               [done[qi][m][1] for m in range(2)])
                s[mp] = s_next

    @pl.when(bounded_ref[0] == 0)
    def _():
        for qi in range(q_per_step):
            m_ref[...] = jnp.full(m_ref.shape, -jnp.inf, jnp.float32)
            l_ref[...] = jnp.zeros(l_ref.shape, jnp.float32)
            acc_ref[...] = jnp.zeros(acc_ref.shape, jnp.float32)

            def body(j, carry):
                start = pl.multiple_of(j * ATTN_K, ATTN_K)
                kt = k_ref[0, pl.ds(start, ATTN_K), :]
                vt = vt_ref[0, 0, :, pl.ds(start, ATTN_K)]
                for mp in range(2):
                    s = jnp.dot(kt, q_tile(qi, mp), preferred_element_type=jnp.float32)
                    m_prev = m_ref[mp]
                    m_new = jnp.maximum(m_prev, jnp.max(s, axis=0, keepdims=True))
                    p = jnp.exp2(s - m_new)
                    alpha = jnp.exp2(m_prev - m_new)
                    l_ref[mp] = alpha * l_ref[mp] + jnp.sum(p, axis=0, keepdims=True)
                    acc_ref[mp] = alpha * acc_ref[mp] + jnp.dot(
                        vt, p.astype(jnp.bfloat16), preferred_element_type=jnp.float32)
                    m_ref[mp] = m_new
                return carry

            lax.fori_loop(0, seq_len // ATTN_K, body, 0)
            finish(qi, [acc_ref[0], acc_ref[1]], [l_ref[0], l_ref[1]])


def _attn_call(lam, bounded, qt, k, vt, subln_col):
    B, S, _ = k.shape
    q_tiles = min(S // ATTN_Q, max(1, ATTN_UNROLLED_TILES // (S // ATTN_K_UNSHIFTED)))
    q_rows = ATTN_Q * q_tiles
    return pl.pallas_call(
        _attn_kernel,
        grid=(B, N_HEADS, S // q_rows),
        in_specs=[
            pl.BlockSpec(memory_space=pltpu.SMEM),
            pl.BlockSpec(memory_space=pltpu.SMEM),
            pl.BlockSpec((1, 1, 2, HEAD_WIDTH, q_rows), lambda b, h, i: (b, h, 0, 0, i)),
            pl.BlockSpec((1, S, HEAD_WIDTH), lambda b, h, i: (b, 0, h)),
            pl.BlockSpec((1, 1, V_DIM, S), lambda b, h, i: (b, h, 0, 0)),
            _const_spec((V_DIM, 1)),
        ],
        out_specs=pl.BlockSpec((1, q_rows, V_DIM), lambda b, h, i: (b, i, h)),
        out_shape=jax.ShapeDtypeStruct((B, S, ATTN_WIDTH), jnp.bfloat16),
        scratch_shapes=[
            pltpu.VMEM((2, 1, ATTN_Q), jnp.float32),
            pltpu.VMEM((2, 1, ATTN_Q), jnp.float32),
            pltpu.VMEM((2, V_DIM, ATTN_Q), jnp.float32),
        ],
        compiler_params=pltpu.CompilerParams(
            dimension_semantics=("parallel", "parallel", "parallel"),
            vmem_limit_bytes=VMEM_LIMIT),
        name="attn",
    )(lam, bounded, qt, k, vt, subln_col)


def _mix_kernel(x_ref, oa_ref, zp_ref, zprev_ref, znext_ref, wpool_ref, pscale_ref, wout_ref,
                y_ref, ext_ref):
    T = MIX_ROWS
    i = pl.program_id(1)
    n_i = pl.num_programs(1)
    seq_len = n_i * T
    ext_ref[pl.ds(0, HALO), :] = jnp.where(i > 0, zprev_ref[0], 0.0)
    ext_ref[pl.ds(HALO, T), :] = zp_ref[0]
    ext_ref[pl.ds(HALO + T, HALO), :] = jnp.where(i < n_i - 1, znext_ref[0], 0.0)
    pos = i * T + lax.broadcasted_iota(jnp.int32, (T, 1), 0)

    y = x_ref[0] + jnp.dot(oa_ref[0], wout_ref[:ATTN_WIDTH, :], preferred_element_type=jnp.float32)

    pooled = []
    for g, w in enumerate(POOL_WINDOWS):
        c0 = g * POOL_GROUP_DIM
        tot = jnp.zeros((T, POOL_GROUP_DIM), jnp.float32)
        for off in range(-(w // 2), w // 2):
            tot = tot + ext_ref[pl.ds(HALO + off, T), c0:c0 + POOL_GROUP_DIM]
        lo = jnp.maximum(pos - w // 2, 0)
        hi = jnp.minimum(pos + w // 2 - 1, seq_len - 1)
        cnt = (hi - lo + 1).astype(jnp.float32)
        pg = tot / cnt - ext_ref[pl.ds(HALO, T), c0:c0 + POOL_GROUP_DIM]
        pw = jnp.dot(pg.astype(jnp.bfloat16), wpool_ref[g], preferred_element_type=jnp.float32)
        pooled.append((pw * pscale_ref[:, c0:c0 + POOL_GROUP_DIM]).astype(jnp.bfloat16))
    o_pool = jnp.concatenate(pooled, axis=-1)
    y_ref[0] = y + jnp.dot(o_pool, wout_ref[ATTN_WIDTH:, :], preferred_element_type=jnp.float32)


def _mix_call(x, o_attn, zp, wpool, pscale, wout):
    B, S, _ = x.shape
    T = MIX_ROWS
    nb = T // HALO
    last = S // HALO - 1
    return pl.pallas_call(
        _mix_kernel,
        grid=(B, S // T),
        in_specs=[
            pl.BlockSpec((1, T, D_MODEL), lambda b, i: (b, i, 0)),
            pl.BlockSpec((1, T, ATTN_WIDTH), lambda b, i: (b, i, 0)),
            pl.BlockSpec((1, T, POOL_WIDTH), lambda b, i: (b, i, 0)),
            pl.BlockSpec((1, HALO, POOL_WIDTH), lambda b, i: (b, jnp.maximum(i * nb - 1, 0), 0)),
            pl.BlockSpec((1, HALO, POOL_WIDTH), lambda b, i: (b, jnp.minimum((i + 1) * nb, last), 0)),
            _const_spec((len(POOL_WINDOWS), POOL_GROUP_DIM, POOL_GROUP_DIM)),
            _const_spec((1, POOL_WIDTH)),
            _const_spec((ATTN_WIDTH + POOL_WIDTH, D_MODEL)),
        ],
        out_specs=pl.BlockSpec((1, T, D_MODEL), lambda b, i: (b, i, 0)),
        out_shape=jax.ShapeDtypeStruct((B, S, D_MODEL), jnp.float32),
        scratch_shapes=[pltpu.VMEM((T + 2 * HALO, POOL_WIDTH), jnp.float32)],
        compiler_params=pltpu.CompilerParams(
            dimension_semantics=("parallel", "parallel"), vmem_limit_bytes=VMEM_LIMIT),
        name="mix",
    )(x, o_attn, zp, zp, zp, wpool, pscale, wout)


def _ffn_kernel(x_ref, xprev_ref, xnext_ref, g2_ref, wup_ref, cw_ref, cb_ref, wdown_ref,
                y_ref, h_ref, act_ref):
    T = FFN_ROWS
    i = pl.program_id(1)
    n_i = pl.num_programs(1)
    g2 = g2_ref[...]
    h_ref[pl.ds(0, HALO), :] = _rms(jnp.where(i > 0, xprev_ref[0], 0.0), g2).astype(jnp.bfloat16)
    h_ref[pl.ds(HALO, T), :] = _rms(x_ref[0], g2).astype(jnp.bfloat16)
    h_ref[pl.ds(HALO + T, HALO), :] = _rms(
        jnp.where(i < n_i - 1, xnext_ref[0], 0.0), g2).astype(jnp.bfloat16)
    h = h_ref[...]

    def conv(u, c0):
        w = cw_ref[:, c0:c0 + FFN_CHUNK]
        return (u[HALO - 1:HALO - 1 + T] * w[0:1] + u[HALO:HALO + T] * w[1:2]
                + u[HALO + 1:HALO + 1 + T] * w[2:3] + cb_ref[:, c0:c0 + FFN_CHUNK])

    def up(c):
        c0 = c * FFN_CHUNK
        return (jnp.dot(h, wup_ref[:, c0:c0 + FFN_CHUNK], preferred_element_type=jnp.float32),
                jnp.dot(h, wup_ref[:, D_FF + c0:D_FF + c0 + FFN_CHUNK],
                        preferred_element_type=jnp.float32))

    n_chunks = D_FF // FFN_CHUNK
    u = up(0)
    for c in range(n_chunks):
        u_next = up(c + 1) if c + 1 < n_chunks else None
        c0 = c * FFN_CHUNK
        half = 0.5 * conv(u[0], c0)
        val = conv(u[1], D_FF + c0)
        act_ref[:, c0:c0 + FFN_CHUNK] = ((half + half * jnp.tanh(half)) * val).astype(jnp.bfloat16)
        u = u_next
    y_ref[0] = x_ref[0] + jnp.dot(act_ref[...], wdown_ref[...],
                                  preferred_element_type=jnp.float32)


def _ffn_call(x, g2, wup, cw, cb, wdown):
    B, S, _ = x.shape
    T = FFN_ROWS
    nb = T // HALO
    last = S // HALO - 1
    single = pl.Buffered(1)
    return pl.pallas_call(
        _ffn_kernel,
        grid=(B, S // T),
        in_specs=[
            pl.BlockSpec((1, T, D_MODEL), lambda b, i: (b, i, 0)),
            pl.BlockSpec((1, HALO, D_MODEL), lambda b, i: (b, jnp.maximum(i * nb - 1, 0), 0)),
            pl.BlockSpec((1, HALO, D_MODEL), lambda b, i: (b, jnp.minimum((i + 1) * nb, last), 0)),
            _const_spec((1, D_MODEL)),
            pl.BlockSpec((D_MODEL, 2 * D_FF), lambda b, i: (0, 0), pipeline_mode=single),
            _const_spec((3, 2 * D_FF)),
            _const_spec((1, 2 * D_FF)),
            pl.BlockSpec((D_FF, D_MODEL), lambda b, i: (0, 0), pipeline_mode=single),
        ],
        out_specs=pl.BlockSpec((1, T, D_MODEL), lambda b, i: (b, i, 0)),
        out_shape=jax.ShapeDtypeStruct((B, S, D_MODEL), jnp.float32),
        scratch_shapes=[
            pltpu.VMEM((T + 2 * HALO, D_MODEL), jnp.bfloat16),
            pltpu.VMEM((T, D_FF), jnp.bfloat16),
        ],
        compiler_params=pltpu.CompilerParams(
            dimension_semantics=("parallel", "parallel"), vmem_limit_bytes=VMEM_LIMIT),
        name="ffn",
    )(x, x, x, g2, wup, cw, cb, wdown)


def _rope_tables(seq_len):
    inv = ROPE_THETA ** (-jnp.arange(0, HEAD_DIM, 2, dtype=jnp.float32) / HEAD_DIM)
    ang = jnp.arange(seq_len, dtype=jnp.float32)[:, None] * inv[None, :]
    ang = jnp.concatenate([ang, ang], axis=-1)
    cos = jnp.cos(ang)
    sin = jnp.sin(ang)
    half = HEAD_DIM // 2
    sin_signed = jnp.concatenate([-sin[:, :half], sin[:, half:]], axis=-1)
    return jnp.tile(cos, (1, 2)), jnp.tile(sin_signed, (1, 2))


def _regroup_w_in(w):
    cols = []
    for hd in range(N_HEADS):
        lo = hd * HEAD_WIDTH
        cols += [w[:, lo:lo + HEAD_WIDTH], w[:, QK_WIDTH + lo:QK_WIDTH + lo + HEAD_WIDTH]]
    cols.append(w[:, 2 * QK_WIDTH:])
    return jnp.concatenate(cols, axis=1)


def _layer(x, p):
    B, S, _ = x.shape
    cos, sin_signed = _rope_tables(S)
    qt, k, vt, zp = _pre_call(x, p["g1"], p["win"], p["qg"], p["kg"], p["gmat"], cos, sin_signed)
    o_attn = _attn_call(p["lam"], p["bounded"], qt, k, vt, p["subln"])
    x1 = _mix_call(x, o_attn, zp, p["wpool"], p["pscale"], p["wout"])
    return _ffn_call(x1, p["g2"], p["wup"], p["cw"], p["cb"], p["wdown"])


def kernel(x_prompt, x_sample, norm1_g, w_in, q_norm_g, k_norm_g, lambda_q1, lambda_k1,
           lambda_q2, lambda_k2, subln_g, w_pool, pool_scale, w_out, norm2_g, w_up, conv_w,
           conv_b, w_down):
    f32 = jnp.float32
    bf16 = jnp.bfloat16
    group = jnp.arange(HEAD_WIDTH) // HEAD_DIM
    lam = (jnp.exp(jnp.sum(lambda_q1[0].astype(f32) * lambda_k1[0].astype(f32)))
           - jnp.exp(jnp.sum(lambda_q2[0].astype(f32) * lambda_k2[0].astype(f32)))
           + LAMBDA_INIT)
    logit_bound = (1.01 * math.sqrt(HEAD_DIM) * jnp.max(jnp.abs(q_norm_g[0].astype(f32)))
                   * jnp.max(jnp.abs(k_norm_g[0].astype(f32))))
    p = {
        "g1": norm1_g[0].reshape(1, D_MODEL),
        "win": _regroup_w_in(w_in[0]).astype(bf16),
        "qg": jnp.tile(q_norm_g[0], 2).reshape(1, HEAD_WIDTH),
        "kg": jnp.tile(k_norm_g[0], 2).reshape(1, HEAD_WIDTH),
        "gmat": (group[:, None] == group[None, :]).astype(bf16),
        "lam": lam.reshape(1),
        "bounded": (logit_bound < MAX_UNSHIFTED_LOGIT).astype(jnp.int32).reshape(1),
        "subln": subln_g[0].reshape(V_DIM, 1),
        "wpool": w_pool[0].astype(bf16),
        "pscale": pool_scale[0].reshape(1, POOL_WIDTH),
        "wout": w_out[0].astype(bf16),
        "g2": norm2_g[0].reshape(1, D_MODEL),
        "wup": w_up[0].astype(bf16),
        "cw": conv_w[0],
        "cb": conv_b[0].reshape(1, 2 * D_FF),
        "wdown": w_down[0].astype(bf16),
    }
    return (_layer(x_prompt, p), _layer(x_sample, p))
```

```python
import functools
import math

import jax
import jax.numpy as jnp
from jax import lax
from jax.experimental import pallas as pl
from jax.experimental.pallas import tpu as pltpu

D_MODEL = 1024
N_HEADS = 4
HEAD_DIM = 64
V_DIM = 2 * HEAD_DIM
HEAD_WIDTH = 2 * HEAD_DIM
ATTN_WIDTH = N_HEADS * V_DIM
QK_WIDTH = N_HEADS * 2 * HEAD_DIM
POOL_WINDOWS = (2, 4, 8, 16)
POOL_GROUP_DIM = 128
POOL_WIDTH = len(POOL_WINDOWS) * POOL_GROUP_DIM
IN_WIDTH = 2 * QK_WIDTH + ATTN_WIDTH + POOL_WIDTH
D_FF = 2816
ROPE_THETA = 10000.0
EPS = 1e-6
LAMBDA_INIT = 0.8 - 0.6 * math.exp(-0.3 * 0)
MAX_UNSHIFTED_LOGIT = 30.0

HALO = 8
VMEM_LIMIT = 56 * 1024 * 1024

PRE_ROWS = 512
PRE_CHUNK = 256
ATTN_Q = 256
ATTN_K = 512
ATTN_UNROLLED_TILES = 128
ATTN_K_UNSHIFTED = 256
ATTN_MAP1_SHIFT = 64
MIX_ROWS = 512
FFN_ROWS = 1024
FFN_CHUNK = 256


def _const_spec(shape):
    return pl.BlockSpec(shape, lambda *_: (0,) * len(shape))


def _rms(x, gain):
    return x * lax.rsqrt(jnp.mean(x * x, axis=-1, keepdims=True) + EPS) * gain


def _group_mean_square(x, gmat):
    sq = x * x
    hi = sq.astype(jnp.bfloat16)
    lo = (sq - hi.astype(jnp.float32)).astype(jnp.bfloat16)
    tot = (jnp.dot(hi, gmat, preferred_element_type=jnp.float32)
           + jnp.dot(lo, gmat, preferred_element_type=jnp.float32))
    return tot * (1.0 / HEAD_DIM)


def _norm_rope(x, gain, gmat, cos, sin_signed, first_half):
    xn = x * lax.rsqrt(_group_mean_square(x, gmat) + EPS) * gain
    rot = jnp.where(first_half,
                    pltpu.roll(xn, HEAD_WIDTH - HEAD_DIM // 2, axis=1),
                    pltpu.roll(xn, HEAD_DIM // 2, axis=1))
    return xn * cos + rot * sin_signed


def _pre_kernel(x_ref, g1_ref, win_ref, qg_ref, kg_ref, gmat_ref, cos_ref, sin_ref,
                qt_ref, k_ref, vt_ref, zp_ref):
    x = x_ref[0]
    h = _rms(x, g1_ref[...]).astype(jnp.bfloat16)
    gmat = gmat_ref[...]
    cos = cos_ref[...]
    sin_signed = sin_ref[...]
    lane = lax.broadcasted_iota(jnp.int32, (1, HEAD_WIDTH), 1)
    first_half = (lane % HEAD_DIM) < (HEAD_DIM // 2)
    row = lax.broadcasted_iota(jnp.int32, (HEAD_WIDTH, 1), 0)
    scale = math.log2(math.e) / math.sqrt(HEAD_DIM)

    def proj(c):
        return jnp.dot(h, win_ref[:, c * PRE_CHUNK:(c + 1) * PRE_CHUNK],
                       preferred_element_type=jnp.float32)

    n_chunks = IN_WIDTH // PRE_CHUNK
    z = proj(0)
    for c in range(n_chunks):
        z_next = proj(c + 1) if c + 1 < n_chunks else None
        if c < N_HEADS:
            q = _norm_rope(z[:, :HEAD_WIDTH], qg_ref[...], gmat, cos, sin_signed, first_half)
            qt = (q * scale).T
            qt_ref[0, c, 0] = jnp.where(row < HEAD_DIM, qt, 0.0).astype(jnp.bfloat16)
            qt_ref[0, c, 1] = jnp.where(row >= HEAD_DIM, qt, 0.0).astype(jnp.bfloat16)
            k = _norm_rope(z[:, HEAD_WIDTH:], kg_ref[...], gmat, cos, sin_signed, first_half)
            k_ref[0, :, c * HEAD_WIDTH:(c + 1) * HEAD_WIDTH] = k.astype(jnp.bfloat16)
        elif c < N_HEADS + N_HEADS // 2:
            for j in range(2):
                hd = 2 * (c - N_HEADS) + j
                vt_ref[0, hd] = z[:, j * V_DIM:(j + 1) * V_DIM].T.astype(jnp.bfloat16)
        else:
            c0 = (c - N_HEADS - N_HEADS // 2) * PRE_CHUNK
            zp_ref[0, :, c0:c0 + PRE_CHUNK] = z
        z = z_next


def _pre_call(x, g1, win, qg, kg, gmat, cos, sin_signed):
    B, S, _ = x.shape
    T = PRE_ROWS
    return pl.pallas_call(
        _pre_kernel,
        grid=(B, S // T),
        in_specs=[
            pl.BlockSpec((1, T, D_MODEL), lambda b, i: (b, i, 0)),
            _const_spec((1, D_MODEL)),
            _const_spec((D_MODEL, IN_WIDTH)),
            _const_spec((1, HEAD_WIDTH)),
            _const_spec((1, HEAD_WIDTH)),
            _const_spec((HEAD_WIDTH, HEAD_WIDTH)),
            pl.BlockSpec((T, HEAD_WIDTH), lambda b, i: (i, 0)),
            pl.BlockSpec((T, HEAD_WIDTH), lambda b, i: (i, 0)),
        ],
        out_specs=[
            pl.BlockSpec((1, N_HEADS, 2, HEAD_WIDTH, T), lambda b, i: (b, 0, 0, 0, i)),
            pl.BlockSpec((1, T, QK_WIDTH), lambda b, i: (b, i, 0)),
            pl.BlockSpec((1, N_HEADS, V_DIM, T), lambda b, i: (b, 0, 0, i)),
            pl.BlockSpec((1, T, POOL_WIDTH), lambda b, i: (b, i, 0)),
        ],
        out_shape=[
            jax.ShapeDtypeStruct((B, N_HEADS, 2, HEAD_WIDTH, S), jnp.bfloat16),
            jax.ShapeDtypeStruct((B, S, QK_WIDTH), jnp.bfloat16),
            jax.ShapeDtypeStruct((B, N_HEADS, V_DIM, S), jnp.bfloat16),
            jax.ShapeDtypeStruct((B, S, POOL_WIDTH), jnp.float32),
        ],
        compiler_params=pltpu.CompilerParams(
            dimension_semantics=("parallel", "parallel"), vmem_limit_bytes=VMEM_LIMIT),
        name="pre",
    )(x, g1, win, qg, kg, gmat, cos, sin_signed)


def _attn_kernel(lam_ref, bounded_ref, qt_ref, k_ref, vt_ref, g_ref, o_ref,
                 m_ref, l_ref, acc_ref):
    seq_len = k_ref.shape[1]
    q_per_step = o_ref.shape[1] // ATTN_Q
    lam = lam_ref[0]

    def finish(qi, acc, l):
        o = acc[0] / l[0] - lam * (acc[1] / l[1])
        o = o * lax.rsqrt(jnp.mean(o * o, axis=0, keepdims=True) + EPS) * g_ref[...]
        o_ref[0, pl.ds(qi * ATTN_Q, ATTN_Q), :] = (o * (1.0 - LAMBDA_INIT)).T.astype(o_ref.dtype)

    def q_tile(qi, mp):
        return qt_ref[0, 0, mp, :, pl.ds(qi * ATTN_Q, ATTN_Q)]

    @pl.when(bounded_ref[0] != 0)
    def _():
        tk = ATTN_K_UNSHIFTED
        bounds = [list(range(0, seq_len + 1, tk)),
                  [0] + list(range(ATTN_MAP1_SHIFT, seq_len, tk)) + [seq_len]]
        work = [[(qi, lo, hi - lo) for qi in range(q_per_step)
                 for lo, hi in zip(bounds[mp][:-1], bounds[mp][1:])] for mp in range(2)]

        def scores(mp, qi, lo, size):
            return jnp.dot(k_ref[0, pl.ds(lo, size), :], q_tile(qi, mp),
                           preferred_element_type=jnp.float32)

        s = [scores(mp, *work[mp][0]) for mp in range(2)]
        l8 = [None, None]
        acc = [None, None]
        done = {}
        for n in range(max(len(w) for w in work)):
            s_next = [scores(mp, *work[mp][n + 1]) if n + 1 < len(work[mp]) else None
                      for mp in range(2)]
            for mp in range(2):
                if n >= len(work[mp]):
                    continue
                qi, lo, size = work[mp][n]
                p = jnp.exp2(s[mp])
                psum = jnp.sum(p.reshape(size // 8, 8, ATTN_Q), axis=0)
                pv = jnp.dot(vt_ref[0, 0, :, pl.ds(lo, size)], p.astype(jnp.bfloat16),
                             preferred_element_type=jnp.float32)
                l8[mp] = psum if lo == 0 else l8[mp] + psum
                acc[mp] = pv if lo == 0 else acc[mp] + pv
                if lo + size == seq_len:
                    done.setdefault(qi, {})[mp] = (acc[mp], jnp.sum(l8[mp], axis=0, keepdims=True))
                    if len(done[qi]) == 2:
                        finish(qi, [done[qi][m][0] for m in range(2)],
                               [done[qi][m][1] for m in range(2)])
            s = s_next

    @pl.when(bounded_ref[0] == 0)
    def _():
        for qi in range(q_per_step):
            m_ref[...] = jnp.full(m_ref.shape, -jnp.inf, jnp.float32)
            l_ref[...] = jnp.zeros(l_ref.shape, jnp.float32)
            acc_ref[...] = jnp.zeros(acc_ref.shape, jnp.float32)

            def body(j, carry):
                start = pl.multiple_of(j * ATTN_K, ATTN_K)
                kt = k_ref[0, pl.ds(start, ATTN_K), :]
                vt = vt_ref[0, 0, :, pl.ds(start, ATTN_K)]
                for mp in range(2):
                    s = jnp.dot(kt, q_tile(qi, mp), preferred_element_type=jnp.float32)
                    m_prev = m_ref[mp]
                    m_new = jnp.maximum(m_prev, jnp.max(s, axis=0, keepdims=True))
                    p = jnp.exp2(s - m_new)
                    alpha = jnp.exp2(m_prev - m_new)
                    l_ref[mp] = alpha * l_ref[mp] + jnp.sum(p, axis=0, keepdims=True)
                    acc_ref[mp] = alpha * acc_ref[mp] + jnp.dot(
                        vt, p.astype(jnp.bfloat16), preferred_element_type=jnp.float32)
                    m_ref[mp] = m_new
                return carry

            lax.fori_loop(0, seq_len // ATTN_K, body, 0)
            finish(qi, [acc_ref[0], acc_ref[1]], [l_ref[0], l_ref[1]])


def _attn_call(lam, bounded, qt, k, vt, subln_col):
    B, S, _ = k.shape
    q_tiles = min(S // ATTN_Q, max(1, ATTN_UNROLLED_TILES // (S // ATTN_K_UNSHIFTED)))
    q_rows = ATTN_Q * q_tiles
    return pl.pallas_call(
        _attn_kernel,
        grid=(B, N_HEADS, S // q_rows),
        in_specs=[
            pl.BlockSpec(memory_space=pltpu.SMEM),
            pl.BlockSpec(memory_space=pltpu.SMEM),
            pl.BlockSpec((1, 1, 2, HEAD_WIDTH, q_rows), lambda b, h, i: (b, h, 0, 0, i)),
            pl.BlockSpec((1, S, HEAD_WIDTH), lambda b, h, i: (b, 0, h)),
            pl.BlockSpec((1, 1, V_DIM, S), lambda b, h, i: (b, h, 0, 0)),
            _const_spec((V_DIM, 1)),
        ],
        out_specs=pl.BlockSpec((1, q_rows, V_DIM), lambda b, h, i: (b, i, h)),
        out_shape=jax.ShapeDtypeStruct((B, S, ATTN_WIDTH), jnp.bfloat16),
        scratch_shapes=[
            pltpu.VMEM((2, 1, ATTN_Q), jnp.float32),
            pltpu.VMEM((2, 1, ATTN_Q), jnp.float32),
            pltpu.VMEM((2, V_DIM, ATTN_Q), jnp.float32),
        ],
        compiler_params=pltpu.CompilerParams(
            dimension_semantics=("parallel", "parallel", "parallel"),
            vmem_limit_bytes=VMEM_LIMIT),
        name="attn",
    )(lam, bounded, qt, k, vt, subln_col)


def _mix_kernel(x_ref, oa_ref, zp_ref, zprev_ref, znext_ref, wpool_ref, pscale_ref, wout_ref,
                y_ref, ext_ref):
    T = MIX_ROWS
    i = pl.program_id(1)
    n_i = pl.num_programs(1)
    seq_len = n_i * T
    ext_ref[pl.ds(0, HALO), :] = jnp.where(i > 0, zprev_ref[0], 0.0)
    ext_ref[pl.ds(HALO, T), :] = zp_ref[0]
    ext_ref[pl.ds(HALO + T, HALO), :] = jnp.where(i < n_i - 1, znext_ref[0], 0.0)
    pos = i * T + lax.broadcasted_iota(jnp.int32, (T, 1), 0)

    y = x_ref[0] + jnp.dot(oa_ref[0], wout_ref[:ATTN_WIDTH, :], preferred_element_type=jnp.float32)

    pooled = []
    for g, w in enumerate(POOL_WINDOWS):
        c0 = g * POOL_GROUP_DIM
        tot = jnp.zeros((T, POOL_GROUP_DIM), jnp.float32)
        for off in range(-(w // 2), w // 2):
            tot = tot + ext_ref[pl.ds(HALO + off, T), c0:c0 + POOL_GROUP_DIM]
        lo = jnp.maximum(pos - w // 2, 0)
        hi = jnp.minimum(pos + w // 2 - 1, seq_len - 1)
        cnt = (hi - lo + 1).astype(jnp.float32)
        pg = tot / cnt - ext_ref[pl.ds(HALO, T), c0:c0 + POOL_GROUP_DIM]
        pw = jnp.dot(pg.astype(jnp.bfloat16), wpool_ref[g], preferred_element_type=jnp.float32)
        pooled.append((pw * pscale_ref[:, c0:c0 + POOL_GROUP_DIM]).astype(jnp.bfloat16))
    o_pool = jnp.concatenate(pooled, axis=-1)
    y_ref[0] = y + jnp.dot(o_pool, wout_ref[ATTN_WIDTH:, :], preferred_element_type=jnp.float32)


def _mix_call(x, o_attn, zp, wpool, pscale, wout):
    B, S, _ = x.shape
    T = MIX_ROWS
    nb = T // HALO
    last = S // HALO - 1
    return pl.pallas_call(
        _mix_kernel,
        grid=(B, S // T),
        in_specs=[
            pl.BlockSpec((1, T, D_MODEL), lambda b, i: (b, i, 0)),
            pl.BlockSpec((1, T, ATTN_WIDTH), lambda b, i: (b, i, 0)),
            pl.BlockSpec((1, T, POOL_WIDTH), lambda b, i: (b, i, 0)),
            pl.BlockSpec((1, HALO, POOL_WIDTH), lambda b, i: (b, jnp.maximum(i * nb - 1, 0), 0)),
            pl.BlockSpec((1, HALO, POOL_WIDTH), lambda b, i: (b, jnp.minimum((i + 1) * nb, last), 0)),
            _const_spec((len(POOL_WINDOWS), POOL_GROUP_DIM, POOL_GROUP_DIM)),
            _const_spec((1, POOL_WIDTH)),
            _const_spec((ATTN_WIDTH + POOL_WIDTH, D_MODEL)),
        ],
        out_specs=pl.BlockSpec((1, T, D_MODEL), lambda b, i: (b, i, 0)),
        out_shape=jax.ShapeDtypeStruct((B, S, D_MODEL), jnp.float32),
        scratch_shapes=[pltpu.VMEM((T + 2 * HALO, POOL_WIDTH), jnp.float32)],
        compiler_params=pltpu.CompilerParams(
            dimension_semantics=("parallel", "parallel"), vmem_limit_bytes=VMEM_LIMIT),
        name="mix",
    )(x, o_attn, zp, zp, zp, wpool, pscale, wout)


def _ffn_kernel(x_ref, xprev_ref, xnext_ref, g2_ref, wup_ref, cw_ref, cb_ref, wdown_ref,
                y_ref, h_ref, act_ref):
    T = FFN_ROWS
    i = pl.program_id(1)
    n_i = pl.num_programs(1)
    g2 = g2_ref[...]
    h_ref[pl.ds(0, HALO), :] = _rms(jnp.where(i > 0, xprev_ref[0], 0.0), g2).astype(jnp.bfloat16)
    h_ref[pl.ds(HALO, T), :] = _rms(x_ref[0], g2).astype(jnp.bfloat16)
    h_ref[pl.ds(HALO + T, HALO), :] = _rms(
        jnp.where(i < n_i - 1, xnext_ref[0], 0.0), g2).astype(jnp.bfloat16)
    h = h_ref[...]

    def conv(u, c0):
        w = cw_ref[:, c0:c0 + FFN_CHUNK]
        return (u[HALO - 1:HALO - 1 + T] * w[0:1] + u[HALO:HALO + T] * w[1:2]
                + u[HALO + 1:HALO + 1 + T] * w[2:3] + cb_ref[:, c0:c0 + FFN_CHUNK])

    def up(c):
        c0 = c * FFN_CHUNK
        return (jnp.dot(h, wup_ref[:, c0:c0 + FFN_CHUNK], preferred_element_type=jnp.float32),
                jnp.dot(h, wup_ref[:, D_FF + c0:D_FF + c0 + FFN_CHUNK],
                        preferred_element_type=jnp.float32))

    n_chunks = D_FF // FFN_CHUNK
    u = up(0)
    for c in range(n_chunks):
        u_next = up(c + 1) if c + 1 < n_chunks else None
        c0 = c * FFN_CHUNK
        half = 0.5 * conv(u[0], c0)
        val = conv(u[1], D_FF + c0)
        act_ref[:, c0:c0 + FFN_CHUNK] = ((half + half * jnp.tanh(half)) * val).astype(jnp.bfloat16)
        u = u_next
    y_ref[0] = x_ref[0] + jnp.dot(act_ref[...], wdown_ref[...],
                                  preferred_element_type=jnp.float32)


def _ffn_call(x, g2, wup, cw, cb, wdown):
    B, S, _ = x.shape
    T = FFN_ROWS
    nb = T // HALO
    last = S // HALO - 1
    single = pl.Buffered(1)
    return pl.pallas_call(
        _ffn_kernel,
        grid=(B, S // T),
        in_specs=[
            pl.BlockSpec((1, T, D_MODEL), lambda b, i: (b, i, 0)),
            pl.BlockSpec((1, HALO, D_MODEL), lambda b, i: (b, jnp.maximum(i * nb - 1, 0), 0)),
            pl.BlockSpec((1, HALO, D_MODEL), lambda b, i: (b, jnp.minimum((i + 1) * nb, last), 0)),
            _const_spec((1, D_MODEL)),
            pl.BlockSpec((D_MODEL, 2 * D_FF), lambda b, i: (0, 0), pipeline_mode=single),
            _const_spec((3, 2 * D_FF)),
            _const_spec((1, 2 * D_FF)),
            pl.BlockSpec((D_FF, D_MODEL), lambda b, i: (0, 0), pipeline_mode=single),
        ],
        out_specs=pl.BlockSpec((1, T, D_MODEL), lambda b, i: (b, i, 0)),
        out_shape=jax.ShapeDtypeStruct((B, S, D_MODEL), jnp.float32),
        scratch_shapes=[
            pltpu.VMEM((T + 2 * HALO, D_MODEL), jnp.bfloat16),
            pltpu.VMEM((T, D_FF), jnp.bfloat16),
        ],
        compiler_params=pltpu.CompilerParams(
            dimension_semantics=("parallel", "parallel"), vmem_limit_bytes=VMEM_LIMIT),
        name="ffn",
    )(x, x, x, g2, wup, cw, cb, wdown)


def _rope_tables(seq_len):
    inv = ROPE_THETA ** (-jnp.arange(0, HEAD_DIM, 2, dtype=jnp.float32) / HEAD_DIM)
    ang = jnp.arange(seq_len, dtype=jnp.float32)[:, None] * inv[None, :]
    ang = jnp.concatenate([ang, ang], axis=-1)
    cos = jnp.cos(ang)
    sin = jnp.sin(ang)
    half = HEAD_DIM // 2
    sin_signed = jnp.concatenate([-sin[:, :half], sin[:, half:]], axis=-1)
    return jnp.tile(cos, (1, 2)), jnp.tile(sin_signed, (1, 2))


def _regroup_w_in(w):
    cols = []
    for hd in range(N_HEADS):
        lo = hd * HEAD_WIDTH
        cols += [w[:, lo:lo + HEAD_WIDTH], w[:, QK_WIDTH + lo:QK_WIDTH + lo + HEAD_WIDTH]]
    cols.append(w[:, 2 * QK_WIDTH:])
    return jnp.concatenate(cols, axis=1)


def _layer(x, p):
    B, S, _ = x.shape
    cos, sin_signed = _rope_tables(S)
    qt, k, vt, zp = _pre_call(x, p["g1"], p["win"], p["qg"], p["kg"], p["gmat"], cos, sin_signed)
    o_attn = _attn_call(p["lam"], p["bounded"], qt, k, vt, p["subln"])
    x1 = _mix_call(x, o_attn, zp, p["wpool"], p["pscale"], p["wout"])
    return _ffn_call(x1, p["g2"], p["wup"], p["cw"], p["cb"], p["wdown"])


def kernel(x_prompt, x_sample, norm1_g, w_in, q_norm_g, k_norm_g, lambda_q1, lambda_k1,
           lambda_q2, lambda_k2, subln_g, w_pool, pool_scale, w_out, norm2_g, w_up, conv_w,
           conv_b, w_down):
    f32 = jnp.float32
    bf16 = jnp.bfloat16
    group = jnp.arange(HEAD_WIDTH) // HEAD_DIM
    lam = (jnp.exp(jnp.sum(lambda_q1[0].astype(f32) * lambda_k1[0].astype(f32)))
           - jnp.exp(jnp.sum(lambda_q2[0].astype(f32) * lambda_k2[0].astype(f32)))
           + LAMBDA_INIT)
    logit_bound = (1.01 * math.sqrt(HEAD_DIM) * jnp.max(jnp.abs(q_norm_g[0].astype(f32)))
                   * jnp.max(jnp.abs(k_norm_g[0].astype(f32))))
    p = {
        "g1": norm1_g[0].reshape(1, D_MODEL),
        "win": _regroup_w_in(w_in[0]).astype(bf16),
        "qg": jnp.tile(q_norm_g[0], 2).reshape(1, HEAD_WIDTH),
        "kg": jnp.tile(k_norm_g[0], 2).reshape(1, HEAD_WIDTH),
        "gmat": (group[:, None] == group[None, :]).astype(bf16),
        "lam": lam.reshape(1),
        "bounded": (logit_bound < MAX_UNSHIFTED_LOGIT).astype(jnp.int32).reshape(1),
        "subln": subln_g[0].reshape(V_DIM, 1),
        "wpool": w_pool[0].astype(bf16),
        "pscale": pool_scale[0].reshape(1, POOL_WIDTH),
        "wout": w_out[0].astype(bf16),
        "g2": norm2_g[0].reshape(1, D_MODEL),
        "wup": w_up[0].astype(bf16),
        "cw": conv_w[0],
        "cb": conv_b[0].reshape(1, 2 * D_FF),
        "wdown": w_down[0].astype(bf16),
    }
    return (_layer(x_prompt, p), _layer(x_sample, p))
```

```python
import functools
import math

import jax
import jax.numpy as jnp
from jax import lax
from jax.experimental import pallas as pl
from jax.experimental.pallas import tpu as pltpu

D_MODEL = 1024
N_HEADS = 4
HEAD_DIM = 64
V_DIM = 2 * HEAD_DIM
HEAD_WIDTH = 2 * HEAD_DIM
ATTN_WIDTH = N_HEADS * V_DIM
QK_WIDTH = N_HEADS * 2 * HEAD_DIM
POOL_WINDOWS = (2, 4, 8, 16)
POOL_GROUP_DIM = 128
POOL_WIDTH = len(POOL_WINDOWS) * POOL_GROUP_DIM
IN_WIDTH = 2 * QK_WIDTH + ATTN_WIDTH + POOL_WIDTH
D_FF = 2816
ROPE_THETA = 10000.0
EPS = 1e-6
LAMBDA_INIT = 0.8 - 0.6 * math.exp(-0.3 * 0)
MAX_UNSHIFTED_LOGIT = 30.0

HALO = 8
VMEM_LIMIT = 56 * 1024 * 1024

PRE_ROWS = 512
PRE_CHUNK = 256
ATTN_Q = 256
ATTN_K = 512
ATTN_UNROLLED_TILES = 128
ATTN_K_UNSHIFTED = 256
ATTN_MAP1_SHIFT = 64
ATTN_LOOKAHEAD = 2
MIX_ROWS = 512
MIX_CHUNK = 256
FFN_ROWS = 1024
FFN_CHUNK = 256


def _const_spec(shape):
    return pl.BlockSpec(shape, lambda *_: (0,) * len(shape))


def _rms(x, gain):
    return x * lax.rsqrt(jnp.mean(x * x, axis=-1, keepdims=True) + EPS) * gain


def _group_mean_square(x, gmat):
    sq = x * x
    hi = sq.astype(jnp.bfloat16)
    lo = (sq - hi.astype(jnp.float32)).astype(jnp.bfloat16)
    tot = (jnp.dot(hi, gmat, preferred_element_type=jnp.float32)
           + jnp.dot(lo, gmat, preferred_element_type=jnp.float32))
    return tot * (1.0 / HEAD_DIM)


def _norm_rope(x, gain, gmat, cos, sin_signed, first_half):
    xn = x * lax.rsqrt(_group_mean_square(x, gmat) + EPS) * gain
    rot = jnp.where(first_half,
                    pltpu.roll(xn, HEAD_WIDTH - HEAD_DIM // 2, axis=1),
                    pltpu.roll(xn, HEAD_DIM // 2, axis=1))
    return xn * cos + rot * sin_signed


def _pre_kernel(x_ref, g1_ref, win_ref, qg_ref, kg_ref, gmat_ref, cos_ref, sin_ref,
                qt_ref, k_ref, vt_ref, zp_ref):
    x = x_ref[0]
    h = _rms(x, g1_ref[...]).astype(jnp.bfloat16)
    gmat = gmat_ref[...]
    cos = cos_ref[...]
    sin_signed = sin_ref[...]
    lane = lax.broadcasted_iota(jnp.int32, (1, HEAD_WIDTH), 1)
    first_half = (lane % HEAD_DIM) < (HEAD_DIM // 2)
    row = lax.broadcasted_iota(jnp.int32, (HEAD_WIDTH, 1), 0)
    scale = math.log2(math.e) / math.sqrt(HEAD_DIM)

    def proj(c):
        return jnp.dot(h, win_ref[:, c * PRE_CHUNK:(c + 1) * PRE_CHUNK],
                       preferred_element_type=jnp.float32)

    n_chunks = IN_WIDTH // PRE_CHUNK
    z = proj(0)
    for c in range(n_chunks):
        z_next = proj(c + 1) if c + 1 < n_chunks else None
        if c < N_HEADS:
            q = _norm_rope(z[:, :HEAD_WIDTH], qg_ref[...], gmat, cos, sin_signed, first_half)
            qt = (q * scale).T
            qt_ref[0, c, 0] = jnp.where(row < HEAD_DIM, qt, 0.0).astype(jnp.bfloat16)
            qt_ref[0, c, 1] = jnp.where(row >= HEAD_DIM, qt, 0.0).astype(jnp.bfloat16)
            k = _norm_rope(z[:, HEAD_WIDTH:], kg_ref[...], gmat, cos, sin_signed, first_half)
            k_ref[0, :, c * HEAD_WIDTH:(c + 1) * HEAD_WIDTH] = k.astype(jnp.bfloat16)
        elif c < N_HEADS + N_HEADS // 2:
            for j in range(2):
                hd = 2 * (c - N_HEADS) + j
                vt_ref[0, hd] = z[:, j * V_DIM:(j + 1) * V_DIM].T.astype(jnp.bfloat16)
        else:
            c0 = (c - N_HEADS - N_HEADS // 2) * PRE_CHUNK
            zp_ref[0, :, c0:c0 + PRE_CHUNK] = z
        z = z_next


def _pre_call(x, g1, win, qg, kg, gmat, cos, sin_signed):
    B, S, _ = x.shape
    T = PRE_ROWS
    return pl.pallas_call(
        _pre_kernel,
        grid=(B, S // T),
        in_specs=[
            pl.BlockSpec((1, T, D_MODEL), lambda b, i: (b, i, 0)),
            _const_spec((1, D_MODEL)),
            _const_spec((D_MODEL, IN_WIDTH)),
            _const_spec((1, HEAD_WIDTH)),
            _const_spec((1, HEAD_WIDTH)),
            _const_spec((HEAD_WIDTH, HEAD_WIDTH)),
            pl.BlockSpec((T, HEAD_WIDTH), lambda b, i: (i, 0)),
            pl.BlockSpec((T, HEAD_WIDTH), lambda b, i: (i, 0)),
        ],
        out_specs=[
            pl.BlockSpec((1, N_HEADS, 2, HEAD_WIDTH, T), lambda b, i: (b, 0, 0, 0, i)),
            pl.BlockSpec((1, T, QK_WIDTH), lambda b, i: (b, i, 0)),
            pl.BlockSpec((1, N_HEADS, V_DIM, T), lambda b, i: (b, 0, 0, i)),
            pl.BlockSpec((1, T, POOL_WIDTH), lambda b, i: (b, i, 0)),
        ],
        out_shape=[
            jax.ShapeDtypeStruct((B, N_HEADS, 2, HEAD_WIDTH, S), jnp.bfloat16),
            jax.ShapeDtypeStruct((B, S, QK_WIDTH), jnp.bfloat16),
            jax.ShapeDtypeStruct((B, N_HEADS, V_DIM, S), jnp.bfloat16),
            jax.ShapeDtypeStruct((B, S, POOL_WIDTH), jnp.float32),
        ],
        compiler_params=pltpu.CompilerParams(
            dimension_semantics=("parallel", "parallel"), vmem_limit_bytes=VMEM_LIMIT),
        name="pre",
    )(x, g1, win, qg, kg, gmat, cos, sin_signed)


def _attn_kernel(lam_ref, bounded_ref, qt_ref, k_ref, vt_ref, g_ref, o_ref,
                 m_ref, l_ref, acc_ref):
    seq_len = k_ref.shape[1]
    q_per_step = o_ref.shape[1] // ATTN_Q
    lam = lam_ref[0]

    def finish(qi, acc, l):
        o = acc[0] / l[0] - lam * (acc[1] / l[1])
        o = o * lax.rsqrt(jnp.mean(o * o, axis=0, keepdims=True) + EPS) * g_ref[...]
        o_ref[0, pl.ds(qi * ATTN_Q, ATTN_Q), :] = (o * (1.0 - LAMBDA_INIT)).T.astype(o_ref.dtype)

    def q_tile(qi, mp):
        return qt_ref[0, 0, mp, :, pl.ds(qi * ATTN_Q, ATTN_Q)]

    @pl.when(bounded_ref[0] != 0)
    def _():
        tk = ATTN_K_UNSHIFTED
        bounds = [list(range(0, seq_len + 1, tk)),
                  [0] + list(range(ATTN_MAP1_SHIFT, seq_len, tk)) + [seq_len]]
        work = [[(qi, lo, hi - lo, lo == 0, hi == seq_len) for qi in range(q_per_step)
                 for lo, hi in zip(bounds[mp][:-1], bounds[mp][1:])] for mp in range(2)]

        def scores(mp, qi, lo, size, first, last):
            return jnp.dot(k_ref[0, pl.ds(lo, size), :], q_tile(qi, mp),
                           preferred_element_type=jnp.float32)

        ahead = ATTN_LOOKAHEAD
        pending = [[scores(mp, *work[mp][n]) for n in range(min(ahead, len(work[mp])))]
                   for mp in range(2)]
        l8 = [None, None]
        acc = [None, None]
        done = {}
        for n in range(max(len(w) for w in work)):
            for mp in range(2):
                if n + ahead < len(work[mp]):
                    pending[mp].append(scores(mp, *work[mp][n + ahead]))
            for mp in range(2):
                if n >= len(work[mp]):
                    continue
                qi, lo, size, first, last = work[mp][n]
                p = jnp.exp2(pending[mp].pop(0))
                psum = jnp.sum(p.reshape(size // 8, 8, ATTN_Q), axis=0)
                pv = jnp.dot(vt_ref[0, 0, :, pl.ds(lo, size)], p.astype(jnp.bfloat16),
                             preferred_element_type=jnp.float32)
                l8[mp] = psum if first else l8[mp] + psum
                acc[mp] = pv if first else acc[mp] + pv
                if last:
                    done.setdefault(qi, {})[mp] = (acc[mp], jnp.sum(l8[mp], axis=0, keepdims=True))
                    if len(done[qi]) == 2:
                        finish(qi, [done[qi][m][0] for m in range(2)],
                               [done[qi][m][1] for m in range(2)])

    @pl.when(bounded_ref[0] == 0)
    def _():
        for qi in range(q_per_step):
            m_ref[...] = jnp.full(m_ref.shape, -jnp.inf, jnp.float32)
            l_ref[...] = jnp.zeros(l_ref.shape, jnp.float32)
            acc_ref[...] = jnp.zeros(acc_ref.shape, jnp.float32)

            def body(j, carry):
                start = pl.multiple_of(j * ATTN_K, ATTN_K)
                kt = k_ref[0, pl.ds(start, ATTN_K), :]
                vt = vt_ref[0, 0, :, pl.ds(start, ATTN_K)]
                for mp in range(2):
                    s = jnp.dot(kt, q_tile(qi, mp), preferred_element_type=jnp.float32)
                    m_prev = m_ref[mp]
                    m_new = jnp.maximum(m_prev, jnp.max(s, axis=0, keepdims=True))
                    p = jnp.exp2(s - m_new)
                    alpha = jnp.exp2(m_prev - m_new)
                    l_ref[mp] = alpha * l_ref[mp] + jnp.sum(p, axis=0, keepdims=True)
                    acc_ref[mp] = alpha * acc_ref[mp] + jnp.dot(
                        vt, p.astype(jnp.bfloat16), preferred_element_type=jnp.float32)
                    m_ref[mp] = m_new
                return carry

            lax.fori_loop(0, seq_len // ATTN_K, body, 0)
            finish(qi, [acc_ref[0], acc_ref[1]], [l_ref[0], l_ref[1]])


def _attn_call(lam, bounded, qt, k, vt, subln_col):
    B, S, _ = k.shape
    q_tiles = min(S // ATTN_Q, max(1, ATTN_UNROLLED_TILES // (S // ATTN_K_UNSHIFTED)))
    q_rows = ATTN_Q * q_tiles
    return pl.pallas_call(
        _attn_kernel,
        grid=(B, N_HEADS, S // q_rows),
        in_specs=[
            pl.BlockSpec(memory_space=pltpu.SMEM),
            pl.BlockSpec(memory_space=pltpu.SMEM),
            pl.BlockSpec((1, 1, 2, HEAD_WIDTH, q_rows), lambda b, h, i: (b, h, 0, 0, i)),
            pl.BlockSpec((1, S, HEAD_WIDTH), lambda b, h, i: (b, 0, h)),
            pl.BlockSpec((1, 1, V_DIM, S), lambda b, h, i: (b, h, 0, 0)),
            _const_spec((V_DIM, 1)),
        ],
        out_specs=pl.BlockSpec((1, q_rows, V_DIM), lambda b, h, i: (b, i, h)),
        out_shape=jax.ShapeDtypeStruct((B, S, ATTN_WIDTH), jnp.bfloat16),
        scratch_shapes=[
            pltpu.VMEM((2, 1, ATTN_Q), jnp.float32),
            pltpu.VMEM((2, 1, ATTN_Q), jnp.float32),
            pltpu.VMEM((2, V_DIM, ATTN_Q), jnp.float32),
        ],
        compiler_params=pltpu.CompilerParams(
            dimension_semantics=("parallel", "parallel", "parallel"),
            vmem_limit_bytes=VMEM_LIMIT),
        name="attn",
    )(lam, bounded, qt, k, vt, subln_col)


def _mix_kernel(x_ref, oa_ref, zp_ref, zprev_ref, znext_ref, wpool_ref, pscale_ref, wout_ref,
                y_ref, ext_ref):
    T = MIX_ROWS
    i = pl.program_id(1)
    n_i = pl.num_programs(1)
    seq_len = n_i * T
    ext_ref[pl.ds(0, HALO), :] = jnp.where(i > 0, zprev_ref[0], 0.0)
    ext_ref[pl.ds(HALO, T), :] = zp_ref[0]
    ext_ref[pl.ds(HALO + T, HALO), :] = jnp.where(i < n_i - 1, znext_ref[0], 0.0)
    R = MIX_CHUNK
    ext_rows = R + 2 * HALO

    def window_sum(e, w):
        tot = e + pltpu.roll(e, 1, axis=0)
        v = 2
        while v < w:
            tot = pltpu.roll(tot, v // 2, axis=0) + pltpu.roll(tot, ext_rows - v // 2, axis=0)
            v *= 2
        return tot

    def pooled_group(r, g):
        w = POOL_WINDOWS[g]
        c0 = g * POOL_GROUP_DIM
        pos = i * T + r * R + lax.broadcasted_iota(jnp.int32, (R, 1), 0)
        e = ext_ref[pl.ds(r * R, ext_rows), c0:c0 + POOL_GROUP_DIM]
        tot = window_sum(e, w)[HALO:HALO + R]
        lo = jnp.maximum(pos - w // 2, 0)
        hi = jnp.minimum(pos + w // 2 - 1, seq_len - 1)
        cnt = (hi - lo + 1).astype(jnp.float32)
        pg = tot / cnt - e[HALO:HALO + R]
        pw = jnp.dot(pg.astype(jnp.bfloat16), wpool_ref[g], preferred_element_type=jnp.float32)
        return (pw * pscale_ref[:, c0:c0 + POOL_GROUP_DIM]).astype(jnp.bfloat16)

    n_groups = len(POOL_WINDOWS)
    n_chunks = T // R
    cols_per_piece = D_MODEL // n_groups
    o_pool = jnp.concatenate([pooled_group(0, g) for g in range(n_groups)], axis=-1)
    for r in range(n_chunks):
        rows = pl.ds(r * R, R)
        oa = oa_ref[0, rows, :]
        next_groups = []
        for j in range(n_groups):
            cols = pl.ds(j * cols_per_piece, cols_per_piece)
            piece = (jnp.dot(oa, wout_ref[:ATTN_WIDTH, cols], preferred_element_type=jnp.float32)
                     + jnp.dot(o_pool, wout_ref[ATTN_WIDTH:, cols],
                               preferred_element_type=jnp.float32))
            if r + 1 < n_chunks:
                next_groups.append(pooled_group(r + 1, j))
            y_ref[0, rows, cols] = x_ref[0, rows, cols] + piece
        if r + 1 < n_chunks:
            o_pool = jnp.concatenate(next_groups, axis=-1)


def _mix_call(x, o_attn, zp, wpool, pscale, wout):
    B, S, _ = x.shape
    T = MIX_ROWS
    nb = T // HALO
    last = S // HALO - 1
    return pl.pallas_call(
        _mix_kernel,
        grid=(B, S // T),
        in_specs=[
            pl.BlockSpec((1, T, D_MODEL), lambda b, i: (b, i, 0)),
            pl.BlockSpec((1, T, ATTN_WIDTH), lambda b, i: (b, i, 0)),
            pl.BlockSpec((1, T, POOL_WIDTH), lambda b, i: (b, i, 0)),
            pl.BlockSpec((1, HALO, POOL_WIDTH), lambda b, i: (b, jnp.maximum(i * nb - 1, 0), 0)),
            pl.BlockSpec((1, HALO, POOL_WIDTH), lambda b, i: (b, jnp.minimum((i + 1) * nb, last), 0)),
            _const_spec((len(POOL_WINDOWS), POOL_GROUP_DIM, POOL_GROUP_DIM)),
            _const_spec((1, POOL_WIDTH)),
            _const_spec((ATTN_WIDTH + POOL_WIDTH, D_MODEL)),
        ],
        out_specs=pl.BlockSpec((1, T, D_MODEL), lambda b, i: (b, i, 0)),
        out_shape=jax.ShapeDtypeStruct((B, S, D_MODEL), jnp.float32),
        scratch_shapes=[pltpu.VMEM((T + 2 * HALO, POOL_WIDTH), jnp.float32)],
        compiler_params=pltpu.CompilerParams(
            dimension_semantics=("parallel", "parallel"), vmem_limit_bytes=VMEM_LIMIT),
        name="mix",
    )(x, o_attn, zp, zp, zp, wpool, pscale, wout)


def _ffn_kernel(x_ref, xprev_ref, xnext_ref, g2_ref, wup_ref, cw_ref, cb_ref, wdown_ref,
                y_ref, h_ref, act_ref):
    T = FFN_ROWS
    i = pl.program_id(1)
    n_i = pl.num_programs(1)
    g2 = g2_ref[...]
    h_ref[pl.ds(0, HALO), :] = _rms(jnp.where(i > 0, xprev_ref[0], 0.0), g2).astype(jnp.bfloat16)
    h_ref[pl.ds(HALO, T), :] = _rms(x_ref[0], g2).astype(jnp.bfloat16)
    h_ref[pl.ds(HALO + T, HALO), :] = _rms(
        jnp.where(i < n_i - 1, xnext_ref[0], 0.0), g2).astype(jnp.bfloat16)
    h = h_ref[...]

    def conv(u, c0):
        w = cw_ref[:, c0:c0 + FFN_CHUNK]
        return (u[HALO - 1:HALO - 1 + T] * w[0:1] + u[HALO:HALO + T] * w[1:2]
                + u[HALO + 1:HALO + 1 + T] * w[2:3] + cb_ref[:, c0:c0 + FFN_CHUNK])

    def up(c):
        c0 = c * FFN_CHUNK
        return (jnp.dot(h, wup_ref[:, c0:c0 + FFN_CHUNK], preferred_element_type=jnp.float32),
                jnp.dot(h, wup_ref[:, D_FF + c0:D_FF + c0 + FFN_CHUNK],
                        preferred_element_type=jnp.float32))

    n_chunks = D_FF // FFN_CHUNK
    u = up(0)
    for c in range(n_chunks):
        u_next = up(c + 1) if c + 1 < n_chunks else None
        c0 = c * FFN_CHUNK
        half = 0.5 * conv(u[0], c0)
        val = conv(u[1], D_FF + c0)
        act_ref[:, c0:c0 + FFN_CHUNK] = ((half + half * jnp.tanh(half)) * val).astype(jnp.bfloat16)
        u = u_next
    y_ref[0] = x_ref[0] + jnp.dot(act_ref[...], wdown_ref[...],
                                  preferred_element_type=jnp.float32)


def _ffn_call(x, g2, wup, cw, cb, wdown):
    B, S, _ = x.shape
    T = FFN_ROWS
    nb = T // HALO
    last = S // HALO - 1
    single = pl.Buffered(1)
    return pl.pallas_call(
        _ffn_kernel,
        grid=(B, S // T),
        in_specs=[
            pl.BlockSpec((1, T, D_MODEL), lambda b, i: (b, i, 0)),
            pl.BlockSpec((1, HALO, D_MODEL), lambda b, i: (b, jnp.maximum(i * nb - 1, 0), 0)),
            pl.BlockSpec((1, HALO, D_MODEL), lambda b, i: (b, jnp.minimum((i + 1) * nb, last), 0)),
            _const_spec((1, D_MODEL)),
            pl.BlockSpec((D_MODEL, 2 * D_FF), lambda b, i: (0, 0), pipeline_mode=single),
            _const_spec((3, 2 * D_FF)),
            _const_spec((1, 2 * D_FF)),
            pl.BlockSpec((D_FF, D_MODEL), lambda b, i: (0, 0), pipeline_mode=single),
        ],
        out_specs=pl.BlockSpec((1, T, D_MODEL), lambda b, i: (b, i, 0)),
        out_shape=jax.ShapeDtypeStruct((B, S, D_MODEL), jnp.float32),
        scratch_shapes=[
            pltpu.VMEM((T + 2 * HALO, D_MODEL), jnp.bfloat16),
            pltpu.VMEM((T, D_FF), jnp.bfloat16),
        ],
        compiler_params=pltpu.CompilerParams(
            dimension_semantics=("parallel", "parallel"), vmem_limit_bytes=VMEM_LIMIT),
        name="ffn",
    )(x, x, x, g2, wup, cw, cb, wdown)


def _rope_tables(seq_len):
    inv = ROPE_THETA ** (-jnp.arange(0, HEAD_DIM, 2, dtype=jnp.float32) / HEAD_DIM)
    ang = jnp.arange(seq_len, dtype=jnp.float32)[:, None] * inv[None, :]
    ang = jnp.concatenate([ang, ang], axis=-1)
    cos = jnp.cos(ang)
    sin = jnp.sin(ang)
    half = HEAD_DIM // 2
    sin_signed = jnp.concatenate([-sin[:, :half], sin[:, half:]], axis=-1)
    return jnp.tile(cos, (1, 2)), jnp.tile(sin_signed, (1, 2))


def _regroup_w_in(w):
    cols = []
    for hd in range(N_HEADS):
        lo = hd * HEAD_WIDTH
        cols += [w[:, lo:lo + HEAD_WIDTH], w[:, QK_WIDTH + lo:QK_WIDTH + lo + HEAD_WIDTH]]
    cols.append(w[:, 2 * QK_WIDTH:])
    return jnp.concatenate(cols, axis=1)


def _layer(x, p):
    B, S, _ = x.shape
    cos, sin_signed = _rope_tables(S)
    qt, k, vt, zp = _pre_call(x, p["g1"], p["win"], p["qg"], p["kg"], p["gmat"], cos, sin_signed)
    o_attn = _attn_call(p["lam"], p["bounded"], qt, k, vt, p["subln"])
    x1 = _mix_call(x, o_attn, zp, p["wpool"], p["pscale"], p["wout"])
    return _ffn_call(x1, p["g2"], p["wup"], p["cw"], p["cb"], p["wdown"])


def kernel(x_prompt, x_sample, norm1_g, w_in, q_norm_g, k_norm_g, lambda_q1, lambda_k1,
           lambda_q2, lambda_k2, subln_g, w_pool, pool_scale, w_out, norm2_g, w_up, conv_w,
           conv_b, w_down):
    f32 = jnp.float32
    bf16 = jnp.bfloat16
    group = jnp.arange(HEAD_WIDTH) // HEAD_DIM
    lam = (jnp.exp(jnp.sum(lambda_q1[0].astype(f32) * lambda_k1[0].astype(f32)))
           - jnp.exp(jnp.sum(lambda_q2[0].astype(f32) * lambda_k2[0].astype(f32)))
           + LAMBDA_INIT)
    logit_bound = (1.01 * math.sqrt(HEAD_DIM) * jnp.max(jnp.abs(q_norm_g[0].astype(f32)))
                   * jnp.max(jnp.abs(k_norm_g[0].astype(f32))))
    p = {
        "g1": norm1_g[0].reshape(1, D_MODEL),
        "win": _regroup_w_in(w_in[0]).astype(bf16),
        "qg": jnp.tile(q_norm_g[0], 2).reshape(1, HEAD_WIDTH),
        "kg": jnp.tile(k_norm_g[0], 2).reshape(1, HEAD_WIDTH),
        "gmat": (group[:, None] == group[None, :]).astype(bf16),
        "lam": lam.reshape(1),
        "bounded": (logit_bound < MAX_UNSHIFTED_LOGIT).astype(jnp.int32).reshape(1),
        "subln": subln_g[0].reshape(V_DIM, 1),
        "wpool": w_pool[0].astype(bf16),
        "pscale": pool_scale[0].reshape(1, POOL_WIDTH),
        "wout": w_out[0].astype(bf16),
        "g2": norm2_g[0].reshape(1, D_MODEL),
        "wup": w_up[0].astype(bf16),
        "cw": conv_w[0],
        "cb": conv_b[0].reshape(1, 2 * D_FF),
        "wdown": w_down[0].astype(bf16),
    }
    return (_layer(x_prompt, p), _layer(x_sample, p))
```

```python
import functools
import math

import jax
import jax.numpy as jnp
from jax import lax
from jax.experimental import pallas as pl
from jax.experimental.pallas import tpu as pltpu

D_MODEL = 1024
N_HEADS = 4
HEAD_DIM = 64
V_DIM = 2 * HEAD_DIM
HEAD_WIDTH = 2 * HEAD_DIM
ATTN_WIDTH = N_HEADS * V_DIM
QK_WIDTH = N_HEADS * 2 * HEAD_DIM
POOL_WINDOWS = (2, 4, 8, 16)
POOL_GROUP_DIM = 128
POOL_WIDTH = len(POOL_WINDOWS) * POOL_GROUP_DIM
IN_WIDTH = 2 * QK_WIDTH + ATTN_WIDTH + POOL_WIDTH
D_FF = 2816
ROPE_THETA = 10000.0
EPS = 1e-6
LAMBDA_INIT = 0.8 - 0.6 * math.exp(-0.3 * 0)
MAX_UNSHIFTED_LOGIT = 30.0

HALO = 8
VMEM_LIMIT = 56 * 1024 * 1024

PRE_ROWS = 512
PRE_CHUNK = 256
ATTN_Q = 256
ATTN_K = 512
ATTN_UNROLLED_TILES = 128
ATTN_K_UNSHIFTED = 256
ATTN_LOOKAHEAD = 2
MIX_ROWS = 512
MIX_CHUNK = 256
FFN_ROWS = 1024
FFN_CHUNK = 256


def _const_spec(shape):
    return pl.BlockSpec(shape, lambda *_: (0,) * len(shape))


def _rms(x, gain):
    return x * lax.rsqrt(jnp.mean(x * x, axis=-1, keepdims=True) + EPS) * gain


def _group_mean_square(x, gmat):
    sq = x * x
    hi = sq.astype(jnp.bfloat16)
    lo = (sq - hi.astype(jnp.float32)).astype(jnp.bfloat16)
    tot = (jnp.dot(hi, gmat, preferred_element_type=jnp.float32)
           + jnp.dot(lo, gmat, preferred_element_type=jnp.float32))
    return tot * (1.0 / HEAD_DIM)


def _norm_rope(x, gain, gmat, cos, sin_signed, first_half):
    xn = x * lax.rsqrt(_group_mean_square(x, gmat) + EPS) * gain
    rot = jnp.where(first_half,
                    pltpu.roll(xn, HEAD_WIDTH - HEAD_DIM // 2, axis=1),
                    pltpu.roll(xn, HEAD_DIM // 2, axis=1))
    return xn * cos + rot * sin_signed


def _pre_kernel(x_ref, g1_ref, win_ref, qg_ref, kg_ref, gmat_ref, cos_ref, sin_ref,
                qt_ref, k_ref, vt_ref, zp_ref):
    x = x_ref[0]
    h = _rms(x, g1_ref[...]).astype(jnp.bfloat16)
    gmat = gmat_ref[...]
    cos = cos_ref[...]
    sin_signed = sin_ref[...]
    lane = lax.broadcasted_iota(jnp.int32, (1, HEAD_WIDTH), 1)
    first_half = (lane % HEAD_DIM) < (HEAD_DIM // 2)
    row = lax.broadcasted_iota(jnp.int32, (HEAD_WIDTH, 1), 0)
    scale = math.log2(math.e) / math.sqrt(HEAD_DIM)

    def proj(c):
        return jnp.dot(h, win_ref[:, c * PRE_CHUNK:(c + 1) * PRE_CHUNK],
                       preferred_element_type=jnp.float32)

    n_chunks = IN_WIDTH // PRE_CHUNK
    z = proj(0)
    for c in range(n_chunks):
        z_next = proj(c + 1) if c + 1 < n_chunks else None
        if c < N_HEADS:
            q = _norm_rope(z[:, :HEAD_WIDTH], qg_ref[...], gmat, cos, sin_signed, first_half)
            qt = (q * scale).T
            qt_ref[0, c, 0] = jnp.where(row < HEAD_DIM, qt, 0.0).astype(jnp.bfloat16)
            qt_ref[0, c, 1] = jnp.where(row >= HEAD_DIM, qt, 0.0).astype(jnp.bfloat16)
            k = _norm_rope(z[:, HEAD_WIDTH:], kg_ref[...], gmat, cos, sin_signed, first_half)
            k_ref[0, :, c * HEAD_WIDTH:(c + 1) * HEAD_WIDTH] = k.astype(jnp.bfloat16)
        elif c < N_HEADS + N_HEADS // 2:
            for j in range(2):
                hd = 2 * (c - N_HEADS) + j
                vt_ref[0, hd] = z[:, j * V_DIM:(j + 1) * V_DIM].T.astype(jnp.bfloat16)
        else:
            c0 = (c - N_HEADS - N_HEADS // 2) * PRE_CHUNK
            zp_ref[0, :, c0:c0 + PRE_CHUNK] = z
        z = z_next


def _pre_call(x, g1, win, qg, kg, gmat, cos, sin_signed):
    B, S, _ = x.shape
    T = PRE_ROWS
    return pl.pallas_call(
        _pre_kernel,
        grid=(B, S // T),
        in_specs=[
            pl.BlockSpec((1, T, D_MODEL), lambda b, i: (b, i, 0)),
            _const_spec((1, D_MODEL)),
            _const_spec((D_MODEL, IN_WIDTH)),
            _const_spec((1, HEAD_WIDTH)),
            _const_spec((1, HEAD_WIDTH)),
            _const_spec((HEAD_WIDTH, HEAD_WIDTH)),
            pl.BlockSpec((T, HEAD_WIDTH), lambda b, i: (i, 0)),
            pl.BlockSpec((T, HEAD_WIDTH), lambda b, i: (i, 0)),
        ],
        out_specs=[
            pl.BlockSpec((1, N_HEADS, 2, HEAD_WIDTH, T), lambda b, i: (b, 0, 0, 0, i)),
            pl.BlockSpec((1, T, QK_WIDTH), lambda b, i: (b, i, 0)),
            pl.BlockSpec((1, N_HEADS, V_DIM, T), lambda b, i: (b, 0, 0, i)),
            pl.BlockSpec((1, T, POOL_WIDTH), lambda b, i: (b, i, 0)),
        ],
        out_shape=[
            jax.ShapeDtypeStruct((B, N_HEADS, 2, HEAD_WIDTH, S), jnp.bfloat16),
            jax.ShapeDtypeStruct((B, S, QK_WIDTH), jnp.bfloat16),
            jax.ShapeDtypeStruct((B, N_HEADS, V_DIM, S), jnp.bfloat16),
            jax.ShapeDtypeStruct((B, S, POOL_WIDTH), jnp.float32),
        ],
        compiler_params=pltpu.CompilerParams(
            dimension_semantics=("parallel", "parallel"), vmem_limit_bytes=VMEM_LIMIT),
        name="pre",
    )(x, g1, win, qg, kg, gmat, cos, sin_signed)


def _attn_kernel(lam_ref, bounded_ref, qt_ref, k_ref, vt_ref, g_ref, o_ref,
                 m_ref, l_ref, acc_ref):
    seq_len = k_ref.shape[1]
    q_per_step = o_ref.shape[1] // ATTN_Q
    lam = lam_ref[0]

    def finish(qi, acc, l):
        o = acc[0] / l[0] - lam * (acc[1] / l[1])
        o = o * lax.rsqrt(jnp.mean(o * o, axis=0, keepdims=True) + EPS) * g_ref[...]
        o_ref[0, pl.ds(qi * ATTN_Q, ATTN_Q), :] = (o * (1.0 - LAMBDA_INIT)).T.astype(o_ref.dtype)

    def q_tile(qi, mp):
        return qt_ref[0, 0, mp, :, pl.ds(qi * ATTN_Q, ATTN_Q)]

    @pl.when(bounded_ref[0] != 0)
    def _():
        tk = ATTN_K_UNSHIFTED
        n_kt = seq_len // tk
        work = [(qi, t) for qi in range(q_per_step) for t in range(n_kt)]

        def scores(qi, t):
            kt = k_ref[0, pl.ds(t * tk, tk), :]
            return [jnp.dot(kt, q_tile(qi, mp), preferred_element_type=jnp.float32)
                    for mp in range(2)]

        ahead = ATTN_LOOKAHEAD
        pending = [scores(*w) for w in work[:ahead]]
        for n, (qi, t) in enumerate(work):
            if n + ahead < len(work):
                pending.append(scores(*work[n + ahead]))
            s = pending.pop(0)
            if t == 0:
                l8 = [None, None]
                acc = [None, None]
            vt = vt_ref[0, 0, :, pl.ds(t * tk, tk)]
            for mp in range(2):
                p = jnp.exp2(s[mp])
                psum = jnp.sum(p.reshape(tk // 8, 8, ATTN_Q), axis=0)
                pv = jnp.dot(vt, p.astype(jnp.bfloat16), preferred_element_type=jnp.float32)
                l8[mp] = psum if t == 0 else l8[mp] + psum
                acc[mp] = pv if t == 0 else acc[mp] + pv
            if t == n_kt - 1:
                finish(qi, acc, [jnp.sum(x, axis=0, keepdims=True) for x in l8])

    @pl.when(bounded_ref[0] == 0)
    def _():
        for qi in range(q_per_step):
            m_ref[...] = jnp.full(m_ref.shape, -jnp.inf, jnp.float32)
            l_ref[...] = jnp.zeros(l_ref.shape, jnp.float32)
            acc_ref[...] = jnp.zeros(acc_ref.shape, jnp.float32)

            def body(j, carry):
                start = pl.multiple_of(j * ATTN_K, ATTN_K)
                kt = k_ref[0, pl.ds(start, ATTN_K), :]
                vt = vt_ref[0, 0, :, pl.ds(start, ATTN_K)]
                for mp in range(2):
                    s = jnp.dot(kt, q_tile(qi, mp), preferred_element_type=jnp.float32)
                    m_prev = m_ref[mp]
                    m_new = jnp.maximum(m_prev, jnp.max(s, axis=0, keepdims=True))
                    p = jnp.exp2(s - m_new)
                    alpha = jnp.exp2(m_prev - m_new)
                    l_ref[mp] = alpha * l_ref[mp] + jnp.sum(p, axis=0, keepdims=True)
                    acc_ref[mp] = alpha * acc_ref[mp] + jnp.dot(
                        vt, p.astype(jnp.bfloat16), preferred_element_type=jnp.float32)
                    m_ref[mp] = m_new
                return carry

            lax.fori_loop(0, seq_len // ATTN_K, body, 0)
            finish(qi, [acc_ref[0], acc_ref[1]], [l_ref[0], l_ref[1]])


def _attn_call(lam, bounded, qt, k, vt, subln_col):
    B, S, _ = k.shape
    q_tiles = min(S // ATTN_Q, max(1, ATTN_UNROLLED_TILES // (S // ATTN_K_UNSHIFTED)))
    q_rows = ATTN_Q * q_tiles
    return pl.pallas_call(
        _attn_kernel,
        grid=(B, N_HEADS, S // q_rows),
        in_specs=[
            pl.BlockSpec(memory_space=pltpu.SMEM),
            pl.BlockSpec(memory_space=pltpu.SMEM),
            pl.BlockSpec((1, 1, 2, HEAD_WIDTH, q_rows), lambda b, h, i: (b, h, 0, 0, i)),
            pl.BlockSpec((1, S, HEAD_WIDTH), lambda b, h, i: (b, 0, h)),
            pl.BlockSpec((1, 1, V_DIM, S), lambda b, h, i: (b, h, 0, 0)),
            _const_spec((V_DIM, 1)),
        ],
        out_specs=pl.BlockSpec((1, q_rows, V_DIM), lambda b, h, i: (b, i, h)),
        out_shape=jax.ShapeDtypeStruct((B, S, ATTN_WIDTH), jnp.bfloat16),
        scratch_shapes=[
            pltpu.VMEM((2, 1, ATTN_Q), jnp.float32),
            pltpu.VMEM((2, 1, ATTN_Q), jnp.float32),
            pltpu.VMEM((2, V_DIM, ATTN_Q), jnp.float32),
        ],
        compiler_params=pltpu.CompilerParams(
            dimension_semantics=("parallel", "parallel", "parallel"),
            vmem_limit_bytes=VMEM_LIMIT),
        name="attn",
    )(lam, bounded, qt, k, vt, subln_col)


def _mix_kernel(x_ref, oa_ref, zp_ref, zprev_ref, znext_ref, wpool_ref, pscale_ref, wout_ref,
                y_ref, ext_ref):
    T = MIX_ROWS
    i = pl.program_id(1)
    n_i = pl.num_programs(1)
    seq_len = n_i * T
    ext_ref[pl.ds(0, HALO), :] = jnp.where(i > 0, zprev_ref[0], 0.0)
    ext_ref[pl.ds(HALO, T), :] = zp_ref[0]
    ext_ref[pl.ds(HALO + T, HALO), :] = jnp.where(i < n_i - 1, znext_ref[0], 0.0)
    R = MIX_CHUNK
    ext_rows = R + 2 * HALO

    def window_sum(e, w):
        tot = e + pltpu.roll(e, 1, axis=0)
        v = 2
        while v < w:
            tot = pltpu.roll(tot, v // 2, axis=0) + pltpu.roll(tot, ext_rows - v // 2, axis=0)
            v *= 2
        return tot

    def pooled_group(r, g):
        w = POOL_WINDOWS[g]
        c0 = g * POOL_GROUP_DIM
        pos = i * T + r * R + lax.broadcasted_iota(jnp.int32, (R, 1), 0)
        e = ext_ref[pl.ds(r * R, ext_rows), c0:c0 + POOL_GROUP_DIM]
        tot = window_sum(e, w)[HALO:HALO + R]
        lo = jnp.maximum(pos - w // 2, 0)
        hi = jnp.minimum(pos + w // 2 - 1, seq_len - 1)
        cnt = (hi - lo + 1).astype(jnp.float32)
        pg = tot / cnt - e[HALO:HALO + R]
        pw = jnp.dot(pg.astype(jnp.bfloat16), wpool_ref[g], preferred_element_type=jnp.float32)
        return (pw * pscale_ref[:, c0:c0 + POOL_GROUP_DIM]).astype(jnp.bfloat16)

    n_groups = len(POOL_WINDOWS)
    n_chunks = T // R
    cols_per_piece = D_MODEL // n_groups
    o_pool = jnp.concatenate([pooled_group(0, g) for g in range(n_groups)], axis=-1)
    for r in range(n_chunks):
        rows = pl.ds(r * R, R)
        oa = oa_ref[0, rows, :]
        next_groups = []
        for j in range(n_groups):
            cols = pl.ds(j * cols_per_piece, cols_per_piece)
            piece = (jnp.dot(oa, wout_ref[:ATTN_WIDTH, cols], preferred_element_type=jnp.float32)
                     + jnp.dot(o_pool, wout_ref[ATTN_WIDTH:, cols],
                               preferred_element_type=jnp.float32))
            if r + 1 < n_chunks:
                next_groups.append(pooled_group(r + 1, j))
            y_ref[0, rows, cols] = x_ref[0, rows, cols] + piece
        if r + 1 < n_chunks:
            o_pool = jnp.concatenate(next_groups, axis=-1)


def _mix_call(x, o_attn, zp, wpool, pscale, wout):
    B, S, _ = x.shape
    T = MIX_ROWS
    nb = T // HALO
    last = S // HALO - 1
    return pl.pallas_call(
        _mix_kernel,
        grid=(B, S // T),
        in_specs=[
            pl.BlockSpec((1, T, D_MODEL), lambda b, i: (b, i, 0)),
            pl.BlockSpec((1, T, ATTN_WIDTH), lambda b, i: (b, i, 0)),
            pl.BlockSpec((1, T, POOL_WIDTH), lambda b, i: (b, i, 0)),
            pl.BlockSpec((1, HALO, POOL_WIDTH), lambda b, i: (b, jnp.maximum(i * nb - 1, 0), 0)),
            pl.BlockSpec((1, HALO, POOL_WIDTH), lambda b, i: (b, jnp.minimum((i + 1) * nb, last), 0)),
            _const_spec((len(POOL_WINDOWS), POOL_GROUP_DIM, POOL_GROUP_DIM)),
            _const_spec((1, POOL_WIDTH)),
            _const_spec((ATTN_WIDTH + POOL_WIDTH, D_MODEL)),
        ],
        out_specs=pl.BlockSpec((1, T, D_MODEL), lambda b, i: (b, i, 0)),
        out_shape=jax.ShapeDtypeStruct((B, S, D_MODEL), jnp.float32),
        scratch_shapes=[pltpu.VMEM((T + 2 * HALO, POOL_WIDTH), jnp.float32)],
        compiler_params=pltpu.CompilerParams(
            dimension_semantics=("parallel", "parallel"), vmem_limit_bytes=VMEM_LIMIT),
        name="mix",
    )(x, o_attn, zp, zp, zp, wpool, pscale, wout)


def _ffn_kernel(x_ref, xprev_ref, xnext_ref, g2_ref, wup_ref, cw_ref, cb_ref, wdown_ref,
                y_ref, h_ref, act_ref):
    T = FFN_ROWS
    i = pl.program_id(1)
    n_i = pl.num_programs(1)
    g2 = g2_ref[...]
    h_ref[pl.ds(0, HALO), :] = _rms(jnp.where(i > 0, xprev_ref[0], 0.0), g2).astype(jnp.bfloat16)
    h_ref[pl.ds(HALO, T), :] = _rms(x_ref[0], g2).astype(jnp.bfloat16)
    h_ref[pl.ds(HALO + T, HALO), :] = _rms(
        jnp.where(i < n_i - 1, xnext_ref[0], 0.0), g2).astype(jnp.bfloat16)
    h = h_ref[...]

    def conv(u, c0):
        w = cw_ref[:, c0:c0 + FFN_CHUNK]
        return (u[HALO - 1:HALO - 1 + T] * w[0:1] + u[HALO:HALO + T] * w[1:2]
                + u[HALO + 1:HALO + 1 + T] * w[2:3] + cb_ref[:, c0:c0 + FFN_CHUNK])

    def up(c):
        c0 = c * FFN_CHUNK
        return (jnp.dot(h, wup_ref[:, c0:c0 + FFN_CHUNK], preferred_element_type=jnp.float32),
                jnp.dot(h, wup_ref[:, D_FF + c0:D_FF + c0 + FFN_CHUNK],
                        preferred_element_type=jnp.float32))

    n_chunks = D_FF // FFN_CHUNK
    u = up(0)
    for c in range(n_chunks):
        u_next = up(c + 1) if c + 1 < n_chunks else None
        c0 = c * FFN_CHUNK
        half = 0.5 * conv(u[0], c0)
        val = conv(u[1], D_FF + c0)
        act_ref[:, c0:c0 + FFN_CHUNK] = ((half + half * jnp.tanh(half)) * val).astype(jnp.bfloat16)
        u = u_next
    y_ref[0] = x_ref[0] + jnp.dot(act_ref[...], wdown_ref[...],
                                  preferred_element_type=jnp.float32)


def _ffn_call(x, g2, wup, cw, cb, wdown):
    B, S, _ = x.shape
    T = FFN_ROWS
    nb = T // HALO
    last = S // HALO - 1
    single = pl.Buffered(1)
    return pl.pallas_call(
        _ffn_kernel,
        grid=(B, S // T),
        in_specs=[
            pl.BlockSpec((1, T, D_MODEL), lambda b, i: (b, i, 0)),
            pl.BlockSpec((1, HALO, D_MODEL), lambda b, i: (b, jnp.maximum(i * nb - 1, 0), 0)),
            pl.BlockSpec((1, HALO, D_MODEL), lambda b, i: (b, jnp.minimum((i + 1) * nb, last), 0)),
            _const_spec((1, D_MODEL)),
            pl.BlockSpec((D_MODEL, 2 * D_FF), lambda b, i: (0, 0), pipeline_mode=single),
            _const_spec((3, 2 * D_FF)),
            _const_spec((1, 2 * D_FF)),
            pl.BlockSpec((D_FF, D_MODEL), lambda b, i: (0, 0), pipeline_mode=single),
        ],
        out_specs=pl.BlockSpec((1, T, D_MODEL), lambda b, i: (b, i, 0)),
        out_shape=jax.ShapeDtypeStruct((B, S, D_MODEL), jnp.float32),
        scratch_shapes=[
            pltpu.VMEM((T + 2 * HALO, D_MODEL), jnp.bfloat16),
            pltpu.VMEM((T, D_FF), jnp.bfloat16),
        ],
        compiler_params=pltpu.CompilerParams(
            dimension_semantics=("parallel", "parallel"), vmem_limit_bytes=VMEM_LIMIT),
        name="ffn",
    )(x, x, x, g2, wup, cw, cb, wdown)


def _rope_tables(seq_len):
    inv = ROPE_THETA ** (-jnp.arange(0, HEAD_DIM, 2, dtype=jnp.float32) / HEAD_DIM)
    ang = jnp.arange(seq_len, dtype=jnp.float32)[:, None] * inv[None, :]
    ang = jnp.concatenate([ang, ang], axis=-1)
    cos = jnp.cos(ang)
    sin = jnp.sin(ang)
    half = HEAD_DIM // 2
    sin_signed = jnp.concatenate([-sin[:, :half], sin[:, half:]], axis=-1)
    return jnp.tile(cos, (1, 2)), jnp.tile(sin_signed, (1, 2))


def _regroup_w_in(w):
    cols = []
    for hd in range(N_HEADS):
        lo = hd * HEAD_WIDTH
        cols += [w[:, lo:lo + HEAD_WIDTH], w[:, QK_WIDTH + lo:QK_WIDTH + lo + HEAD_WIDTH]]
    cols.append(w[:, 2 * QK_WIDTH:])
    return jnp.concatenate(cols, axis=1)


def _layer(x, p):
    B, S, _ = x.shape
    cos, sin_signed = _rope_tables(S)
    qt, k, vt, zp = _pre_call(x, p["g1"], p["win"], p["qg"], p["kg"], p["gmat"], cos, sin_signed)
    o_attn = _attn_call(p["lam"], p["bounded"], qt, k, vt, p["subln"])
    x1 = _mix_call(x, o_attn, zp, p["wpool"], p["pscale"], p["wout"])
    return _ffn_call(x1, p["g2"], p["wup"], p["cw"], p["cb"], p["wdown"])


def kernel(x_prompt, x_sample, norm1_g, w_in, q_norm_g, k_norm_g, lambda_q1, lambda_k1,
           lambda_q2, lambda_k2, subln_g, w_pool, pool_scale, w_out, norm2_g, w_up, conv_w,
           conv_b, w_down):
    f32 = jnp.float32
    bf16 = jnp.bfloat16
    group = jnp.arange(HEAD_WIDTH) // HEAD_DIM
    lam = (jnp.exp(jnp.sum(lambda_q1[0].astype(f32) * lambda_k1[0].astype(f32)))
           - jnp.exp(jnp.sum(lambda_q2[0].astype(f32) * lambda_k2[0].astype(f32)))
           + LAMBDA_INIT)
    logit_bound = (1.01 * math.sqrt(HEAD_DIM) * jnp.max(jnp.abs(q_norm_g[0].astype(f32)))
                   * jnp.max(jnp.abs(k_norm_g[0].astype(f32))))
    p = {
        "g1": norm1_g[0].reshape(1, D_MODEL),
        "win": _regroup_w_in(w_in[0]).astype(bf16),
        "qg": jnp.tile(q_norm_g[0], 2).reshape(1, HEAD_WIDTH),
        "kg": jnp.tile(k_norm_g[0], 2).reshape(1, HEAD_WIDTH),
        "gmat": (group[:, None] == group[None, :]).astype(bf16),
        "lam": lam.reshape(1),
        "bounded": (logit_bound < MAX_UNSHIFTED_LOGIT).astype(jnp.int32).reshape(1),
        "subln": subln_g[0].reshape(V_DIM, 1),
        "wpool": w_pool[0].astype(bf16),
        "pscale": pool_scale[0].reshape(1, POOL_WIDTH),
        "wout": w_out[0].astype(bf16),
        "g2": norm2_g[0].reshape(1, D_MODEL),
        "wup": w_up[0].astype(bf16),
        "cw": conv_w[0],
        "cb": conv_b[0].reshape(1, 2 * D_FF),
        "wdown": w_down[0].astype(bf16),
    }
    return (_layer(x_prompt, p), _layer(x_sample, p))
```

```python
import functools
import math

import jax
import jax.numpy as jnp
from jax import lax
from jax.experimental import pallas as pl
from jax.experimental.pallas import tpu as pltpu

D_MODEL = 1024
N_HEADS = 4
HEAD_DIM = 64
V_DIM = 2 * HEAD_DIM
HEAD_WIDTH = 2 * HEAD_DIM
ATTN_WIDTH = N_HEADS * V_DIM
QK_WIDTH = N_HEADS * 2 * HEAD_DIM
POOL_WINDOWS = (2, 4, 8, 16)
POOL_GROUP_DIM = 128
POOL_WIDTH = len(POOL_WINDOWS) * POOL_GROUP_DIM
IN_WIDTH = 2 * QK_WIDTH + ATTN_WIDTH + POOL_WIDTH
D_FF = 2816
ROPE_THETA = 10000.0
EPS = 1e-6
LAMBDA_INIT = 0.8 - 0.6 * math.exp(-0.3 * 0)
MAX_UNSHIFTED_LOGIT = 30.0

HALO = 8
VMEM_LIMIT = 56 * 1024 * 1024

PRE_ROWS = 1024
PRE_CHUNK = 256
ATTN_Q = 256
ATTN_K = 512
ATTN_UNROLLED_TILES = 128
ATTN_K_UNSHIFTED = 256
ATTN_LOOKAHEAD = 2
MIX_ROWS = 1024
MIX_CHUNK = 256
FFN_ROWS = 1024
FFN_CHUNK = 256


def _const_spec(shape):
    return pl.BlockSpec(shape, lambda *_: (0,) * len(shape))


def _rms(x, gain):
    return x * lax.rsqrt(jnp.mean(x * x, axis=-1, keepdims=True) + EPS) * gain


def _group_mean_square(x, gmat):
    sq = x * x
    hi = sq.astype(jnp.bfloat16)
    lo = (sq - hi.astype(jnp.float32)).astype(jnp.bfloat16)
    tot = (jnp.dot(hi, gmat, preferred_element_type=jnp.float32)
           + jnp.dot(lo, gmat, preferred_element_type=jnp.float32))
    return tot * (1.0 / HEAD_DIM)


def _norm_rope(x, gain, gmat, cos, sin_signed, first_half):
    xn = x * lax.rsqrt(_group_mean_square(x, gmat) + EPS) * gain
    rot = jnp.where(first_half,
                    pltpu.roll(xn, HEAD_WIDTH - HEAD_DIM // 2, axis=1),
                    pltpu.roll(xn, HEAD_DIM // 2, axis=1))
    return xn * cos + rot * sin_signed


def _pre_kernel(x_ref, g1_ref, win_ref, qg_ref, kg_ref, gmat_ref, cos_ref, sin_ref,
                qt_ref, k_ref, vt_ref, zp_ref):
    x = x_ref[0]
    h = _rms(x, g1_ref[...]).astype(jnp.bfloat16)
    gmat = gmat_ref[...]
    cos = cos_ref[...]
    sin_signed = sin_ref[...]
    lane = lax.broadcasted_iota(jnp.int32, (1, HEAD_WIDTH), 1)
    first_half = (lane % HEAD_DIM) < (HEAD_DIM // 2)
    row = lax.broadcasted_iota(jnp.int32, (HEAD_WIDTH, 1), 0)
    scale = math.log2(math.e) / math.sqrt(HEAD_DIM)

    def proj(c):
        return jnp.dot(h, win_ref[:, c * PRE_CHUNK:(c + 1) * PRE_CHUNK],
                       preferred_element_type=jnp.float32)

    n_chunks = IN_WIDTH // PRE_CHUNK
    z = proj(0)
    for c in range(n_chunks):
        z_next = proj(c + 1) if c + 1 < n_chunks else None
        if c < N_HEADS:
            q = _norm_rope(z[:, :HEAD_WIDTH], qg_ref[...], gmat, cos, sin_signed, first_half)
            qt = (q * scale).T
            qt_ref[0, c, 0] = jnp.where(row < HEAD_DIM, qt, 0.0).astype(jnp.bfloat16)
            qt_ref[0, c, 1] = jnp.where(row >= HEAD_DIM, qt, 0.0).astype(jnp.bfloat16)
            k = _norm_rope(z[:, HEAD_WIDTH:], kg_ref[...], gmat, cos, sin_signed, first_half)
            k_ref[0, :, c * HEAD_WIDTH:(c + 1) * HEAD_WIDTH] = k.astype(jnp.bfloat16)
        elif c < N_HEADS + N_HEADS // 2:
            for j in range(2):
                hd = 2 * (c - N_HEADS) + j
                vt_ref[0, hd] = z[:, j * V_DIM:(j + 1) * V_DIM].T.astype(jnp.bfloat16)
        else:
            c0 = (c - N_HEADS - N_HEADS // 2) * PRE_CHUNK
            zp_ref[0, :, c0:c0 + PRE_CHUNK] = z
        z = z_next


def _pre_call(x, g1, win, qg, kg, gmat, cos, sin_signed):
    B, S, _ = x.shape
    T = PRE_ROWS
    return pl.pallas_call(
        _pre_kernel,
        grid=(B, S // T),
        in_specs=[
            pl.BlockSpec((1, T, D_MODEL), lambda b, i: (b, i, 0)),
            _const_spec((1, D_MODEL)),
            _const_spec((D_MODEL, IN_WIDTH)),
            _const_spec((1, HEAD_WIDTH)),
            _const_spec((1, HEAD_WIDTH)),
            _const_spec((HEAD_WIDTH, HEAD_WIDTH)),
            pl.BlockSpec((T, HEAD_WIDTH), lambda b, i: (i, 0)),
            pl.BlockSpec((T, HEAD_WIDTH), lambda b, i: (i, 0)),
        ],
        out_specs=[
            pl.BlockSpec((1, N_HEADS, 2, HEAD_WIDTH, T), lambda b, i: (b, 0, 0, 0, i)),
            pl.BlockSpec((1, T, QK_WIDTH), lambda b, i: (b, i, 0)),
            pl.BlockSpec((1, N_HEADS, V_DIM, T), lambda b, i: (b, 0, 0, i)),
            pl.BlockSpec((1, T, POOL_WIDTH), lambda b, i: (b, i, 0)),
        ],
        out_shape=[
            jax.ShapeDtypeStruct((B, N_HEADS, 2, HEAD_WIDTH, S), jnp.bfloat16),
            jax.ShapeDtypeStruct((B, S, QK_WIDTH), jnp.bfloat16),
            jax.ShapeDtypeStruct((B, N_HEADS, V_DIM, S), jnp.bfloat16),
            jax.ShapeDtypeStruct((B, S, POOL_WIDTH), jnp.float32),
        ],
        compiler_params=pltpu.CompilerParams(
            dimension_semantics=("parallel", "parallel"), vmem_limit_bytes=VMEM_LIMIT),
        name="pre",
    )(x, g1, win, qg, kg, gmat, cos, sin_signed)


def _attn_kernel(lam_ref, bounded_ref, qt_ref, k_ref, vt_ref, g_ref, o_ref,
                 m_ref, l_ref, acc_ref):
    seq_len = k_ref.shape[1]
    q_per_step = o_ref.shape[1] // ATTN_Q
    lam = lam_ref[0]

    def finish(qi, acc, l):
        o = acc[0] / l[0] - lam * (acc[1] / l[1])
        o = o * lax.rsqrt(jnp.mean(o * o, axis=0, keepdims=True) + EPS) * g_ref[...]
        o_ref[0, pl.ds(qi * ATTN_Q, ATTN_Q), :] = (o * (1.0 - LAMBDA_INIT)).T.astype(o_ref.dtype)

    def q_tile(qi, mp):
        return qt_ref[0, 0, mp, :, pl.ds(qi * ATTN_Q, ATTN_Q)]

    @pl.when(bounded_ref[0] != 0)
    def _():
        tk = ATTN_K_UNSHIFTED
        n_kt = seq_len // tk
        work = [(qi, t) for qi in range(q_per_step) for t in range(n_kt)]

        def scores(qi, t):
            kt = k_ref[0, pl.ds(t * tk, tk), :]
            return [jnp.dot(kt, q_tile(qi, mp), preferred_element_type=jnp.float32)
                    for mp in range(2)]

        ahead = ATTN_LOOKAHEAD
        pending = [scores(*w) for w in work[:ahead]]
        for n, (qi, t) in enumerate(work):
            if n + ahead < len(work):
                pending.append(scores(*work[n + ahead]))
            s = pending.pop(0)
            if t == 0:
                l8 = [None, None]
                acc = [None, None]
            vt = vt_ref[0, 0, :, pl.ds(t * tk, tk)]
            for mp in range(2):
                p = jnp.exp2(s[mp])
                psum = jnp.sum(p.reshape(tk // 8, 8, ATTN_Q), axis=0)
                pv = jnp.dot(vt, p.astype(jnp.bfloat16), preferred_element_type=jnp.float32)
                l8[mp] = psum if t == 0 else l8[mp] + psum
                acc[mp] = pv if t == 0 else acc[mp] + pv
            if t == n_kt - 1:
                finish(qi, acc, [jnp.sum(x, axis=0, keepdims=True) for x in l8])

    @pl.when(bounded_ref[0] == 0)
    def _():
        for qi in range(q_per_step):
            m_ref[...] = jnp.full(m_ref.shape, -jnp.inf, jnp.float32)
            l_ref[...] = jnp.zeros(l_ref.shape, jnp.float32)
            acc_ref[...] = jnp.zeros(acc_ref.shape, jnp.float32)

            def body(j, carry):
                start = pl.multiple_of(j * ATTN_K, ATTN_K)
                kt = k_ref[0, pl.ds(start, ATTN_K), :]
                vt = vt_ref[0, 0, :, pl.ds(start, ATTN_K)]
                for mp in range(2):
                    s = jnp.dot(kt, q_tile(qi, mp), preferred_element_type=jnp.float32)
                    m_prev = m_ref[mp]
                    m_new = jnp.maximum(m_prev, jnp.max(s, axis=0, keepdims=True))
                    p = jnp.exp2(s - m_new)
                    alpha = jnp.exp2(m_prev - m_new)
                    l_ref[mp] = alpha * l_ref[mp] + jnp.sum(p, axis=0, keepdims=True)
                    acc_ref[mp] = alpha * acc_ref[mp] + jnp.dot(
                        vt, p.astype(jnp.bfloat16), preferred_element_type=jnp.float32)
                    m_ref[mp] = m_new
                return carry

            lax.fori_loop(0, seq_len // ATTN_K, body, 0)
            finish(qi, [acc_ref[0], acc_ref[1]], [l_ref[0], l_ref[1]])


def _attn_call(lam, bounded, qt, k, vt, subln_col):
    B, S, _ = k.shape
    q_tiles = min(S // ATTN_Q, max(1, ATTN_UNROLLED_TILES // (S // ATTN_K_UNSHIFTED)))
    q_rows = ATTN_Q * q_tiles
    return pl.pallas_call(
        _attn_kernel,
        grid=(B, N_HEADS, S // q_rows),
        in_specs=[
            pl.BlockSpec(memory_space=pltpu.SMEM),
            pl.BlockSpec(memory_space=pltpu.SMEM),
            pl.BlockSpec((1, 1, 2, HEAD_WIDTH, q_rows), lambda b, h, i: (b, h, 0, 0, i)),
            pl.BlockSpec((1, S, HEAD_WIDTH), lambda b, h, i: (b, 0, h)),
            pl.BlockSpec((1, 1, V_DIM, S), lambda b, h, i: (b, h, 0, 0)),
            _const_spec((V_DIM, 1)),
        ],
        out_specs=pl.BlockSpec((1, q_rows, V_DIM), lambda b, h, i: (b, i, h)),
        out_shape=jax.ShapeDtypeStruct((B, S, ATTN_WIDTH), jnp.bfloat16),
        scratch_shapes=[
            pltpu.VMEM((2, 1, ATTN_Q), jnp.float32),
            pltpu.VMEM((2, 1, ATTN_Q), jnp.float32),
            pltpu.VMEM((2, V_DIM, ATTN_Q), jnp.float32),
        ],
        compiler_params=pltpu.CompilerParams(
            dimension_semantics=("parallel", "parallel", "parallel"),
            vmem_limit_bytes=VMEM_LIMIT),
        name="attn",
    )(lam, bounded, qt, k, vt, subln_col)


def _mix_kernel(x_ref, oa_ref, zp_ref, zprev_ref, znext_ref, wpool_ref, pscale_ref, wout_ref,
                y_ref, ext_ref):
    T = MIX_ROWS
    i = pl.program_id(1)
    n_i = pl.num_programs(1)
    seq_len = n_i * T
    ext_ref[pl.ds(0, HALO), :] = jnp.where(i > 0, zprev_ref[0], 0.0)
    ext_ref[pl.ds(HALO, T), :] = zp_ref[0]
    ext_ref[pl.ds(HALO + T, HALO), :] = jnp.where(i < n_i - 1, znext_ref[0], 0.0)
    R = MIX_CHUNK
    ext_rows = R + 2 * HALO

    def window_sum(e, w):
        tot = e + pltpu.roll(e, 1, axis=0)
        v = 2
        while v < w:
            tot = pltpu.roll(tot, v // 2, axis=0) + pltpu.roll(tot, ext_rows - v // 2, axis=0)
            v *= 2
        return tot

    def pooled_group(r, g):
        w = POOL_WINDOWS[g]
        c0 = g * POOL_GROUP_DIM
        pos = i * T + r * R + lax.broadcasted_iota(jnp.int32, (R, 1), 0)
        e = ext_ref[pl.ds(r * R, ext_rows), c0:c0 + POOL_GROUP_DIM]
        tot = window_sum(e, w)[HALO:HALO + R]
        lo = jnp.maximum(pos - w // 2, 0)
        hi = jnp.minimum(pos + w // 2 - 1, seq_len - 1)
        cnt = (hi - lo + 1).astype(jnp.float32)
        pg = tot / cnt - e[HALO:HALO + R]
        pw = jnp.dot(pg.astype(jnp.bfloat16), wpool_ref[g], preferred_element_type=jnp.float32)
        return (pw * pscale_ref[:, c0:c0 + POOL_GROUP_DIM]).astype(jnp.bfloat16)

    n_groups = len(POOL_WINDOWS)
    n_chunks = T // R
    cols_per_piece = D_MODEL // n_groups
    o_pool = jnp.concatenate([pooled_group(0, g) for g in range(n_groups)], axis=-1)
    for r in range(n_chunks):
        rows = pl.ds(r * R, R)
        oa = oa_ref[0, rows, :]
        next_groups = []
        for j in range(n_groups):
            cols = pl.ds(j * cols_per_piece, cols_per_piece)
            piece = (jnp.dot(oa, wout_ref[:ATTN_WIDTH, cols], preferred_element_type=jnp.float32)
                     + jnp.dot(o_pool, wout_ref[ATTN_WIDTH:, cols],
                               preferred_element_type=jnp.float32))
            if r + 1 < n_chunks:
                next_groups.append(pooled_group(r + 1, j))
            y_ref[0, rows, cols] = x_ref[0, rows, cols] + piece
        if r + 1 < n_chunks:
            o_pool = jnp.concatenate(next_groups, axis=-1)


def _mix_call(x, o_attn, zp, wpool, pscale, wout):
    B, S, _ = x.shape
    T = MIX_ROWS
    nb = T // HALO
    last = S // HALO - 1
    return pl.pallas_call(
        _mix_kernel,
        grid=(B, S // T),
        in_specs=[
            pl.BlockSpec((1, T, D_MODEL), lambda b, i: (b, i, 0)),
            pl.BlockSpec((1, T, ATTN_WIDTH), lambda b, i: (b, i, 0)),
            pl.BlockSpec((1, T, POOL_WIDTH), lambda b, i: (b, i, 0)),
            pl.BlockSpec((1, HALO, POOL_WIDTH), lambda b, i: (b, jnp.maximum(i * nb - 1, 0), 0)),
            pl.BlockSpec((1, HALO, POOL_WIDTH), lambda b, i: (b, jnp.minimum((i + 1) * nb, last), 0)),
            _const_spec((len(POOL_WINDOWS), POOL_GROUP_DIM, POOL_GROUP_DIM)),
            _const_spec((1, POOL_WIDTH)),
            _const_spec((ATTN_WIDTH + POOL_WIDTH, D_MODEL)),
        ],
        out_specs=pl.BlockSpec((1, T, D_MODEL), lambda b, i: (b, i, 0)),
        out_shape=jax.ShapeDtypeStruct((B, S, D_MODEL), jnp.float32),
        scratch_shapes=[pltpu.VMEM((T + 2 * HALO, POOL_WIDTH), jnp.float32)],
        compiler_params=pltpu.CompilerParams(
            dimension_semantics=("parallel", "parallel"), vmem_limit_bytes=VMEM_LIMIT),
        name="mix",
    )(x, o_attn, zp, zp, zp, wpool, pscale, wout)


def _ffn_kernel(x_ref, xprev_ref, xnext_ref, g2_ref, wup_ref, cw_ref, cb_ref, wdown_ref,
                y_ref, h_ref, act_ref):
    T = FFN_ROWS
    i = pl.program_id(1)
    n_i = pl.num_programs(1)
    g2 = g2_ref[...]
    h_ref[pl.ds(0, HALO), :] = _rms(jnp.where(i > 0, xprev_ref[0], 0.0), g2).astype(jnp.bfloat16)
    h_ref[pl.ds(HALO, T), :] = _rms(x_ref[0], g2).astype(jnp.bfloat16)
    h_ref[pl.ds(HALO + T, HALO), :] = _rms(
        jnp.where(i < n_i - 1, xnext_ref[0], 0.0), g2).astype(jnp.bfloat16)
    h = h_ref[...]

    def conv(u, c0):
        w = cw_ref[:, c0:c0 + FFN_CHUNK]
        return (u[HALO - 1:HALO - 1 + T] * w[0:1] + u[HALO:HALO + T] * w[1:2]
                + u[HALO + 1:HALO + 1 + T] * w[2:3] + cb_ref[:, c0:c0 + FFN_CHUNK])

    def up(c):
        c0 = c * FFN_CHUNK
        return (jnp.dot(h, wup_ref[:, c0:c0 + FFN_CHUNK], preferred_element_type=jnp.float32),
                jnp.dot(h, wup_ref[:, D_FF + c0:D_FF + c0 + FFN_CHUNK],
                        preferred_element_type=jnp.float32))

    n_chunks = D_FF // FFN_CHUNK
    u = up(0)
    for c in range(n_chunks):
        u_next = up(c + 1) if c + 1 < n_chunks else None
        c0 = c * FFN_CHUNK
        half = 0.5 * conv(u[0], c0)
        val = conv(u[1], D_FF + c0)
        act_ref[:, c0:c0 + FFN_CHUNK] = ((half + half * jnp.tanh(half)) * val).astype(jnp.bfloat16)
        u = u_next
    y_ref[0] = x_ref[0] + jnp.dot(act_ref[...], wdown_ref[...],
                                  preferred_element_type=jnp.float32)


def _ffn_call(x, g2, wup, cw, cb, wdown):
    B, S, _ = x.shape
    T = FFN_ROWS
    nb = T // HALO
    last = S // HALO - 1
    single = pl.Buffered(1)
    return pl.pallas_call(
        _ffn_kernel,
        grid=(B, S // T),
        in_specs=[
            pl.BlockSpec((1, T, D_MODEL), lambda b, i: (b, i, 0)),
            pl.BlockSpec((1, HALO, D_MODEL), lambda b, i: (b, jnp.maximum(i * nb - 1, 0), 0)),
            pl.BlockSpec((1, HALO, D_MODEL), lambda b, i: (b, jnp.minimum((i + 1) * nb, last), 0)),
            _const_spec((1, D_MODEL)),
            pl.BlockSpec((D_MODEL, 2 * D_FF), lambda b, i: (0, 0), pipeline_mode=single),
            _const_spec((3, 2 * D_FF)),
            _const_spec((1, 2 * D_FF)),
            pl.BlockSpec((D_FF, D_MODEL), lambda b, i: (0, 0), pipeline_mode=single),
        ],
        out_specs=pl.BlockSpec((1, T, D_MODEL), lambda b, i: (b, i, 0)),
        out_shape=jax.ShapeDtypeStruct((B, S, D_MODEL), jnp.float32),
        scratch_shapes=[
            pltpu.VMEM((T + 2 * HALO, D_MODEL), jnp.bfloat16),
            pltpu.VMEM((T, D_FF), jnp.bfloat16),
        ],
        compiler_params=pltpu.CompilerParams(
            dimension_semantics=("parallel", "parallel"), vmem_limit_bytes=VMEM_LIMIT),
        name="ffn",
    )(x, x, x, g2, wup, cw, cb, wdown)


def _rope_tables(seq_len):
    inv = ROPE_THETA ** (-jnp.arange(0, HEAD_DIM, 2, dtype=jnp.float32) / HEAD_DIM)
    ang = jnp.arange(seq_len, dtype=jnp.float32)[:, None] * inv[None, :]
    ang = jnp.concatenate([ang, ang], axis=-1)
    cos = jnp.cos(ang)
    sin = jnp.sin(ang)
    half = HEAD_DIM // 2
    sin_signed = jnp.concatenate([-sin[:, :half], sin[:, half:]], axis=-1)
    return jnp.tile(cos, (1, 2)), jnp.tile(sin_signed, (1, 2))


def _regroup_w_in(w):
    cols = []
    for hd in range(N_HEADS):
        lo = hd * HEAD_WIDTH
        cols += [w[:, lo:lo + HEAD_WIDTH], w[:, QK_WIDTH + lo:QK_WIDTH + lo + HEAD_WIDTH]]
    cols.append(w[:, 2 * QK_WIDTH:])
    return jnp.concatenate(cols, axis=1)


def _layer(x, p):
    B, S, _ = x.shape
    cos, sin_signed = _rope_tables(S)
    qt, k, vt, zp = _pre_call(x, p["g1"], p["win"], p["qg"], p["kg"], p["gmat"], cos, sin_signed)
    o_attn = _attn_call(p["lam"], p["bounded"], qt, k, vt, p["subln"])
    x1 = _mix_call(x, o_attn, zp, p["wpool"], p["pscale"], p["wout"])
    return _ffn_call(x1, p["g2"], p["wup"], p["cw"], p["cb"], p["wdown"])


def kernel(x_prompt, x_sample, norm1_g, w_in, q_norm_g, k_norm_g, lambda_q1, lambda_k1,
           lambda_q2, lambda_k2, subln_g, w_pool, pool_scale, w_out, norm2_g, w_up, conv_w,
           conv_b, w_down):
    f32 = jnp.float32
    bf16 = jnp.bfloat16
    group = jnp.arange(HEAD_WIDTH) // HEAD_DIM
    lam = (jnp.exp(jnp.sum(lambda_q1[0].astype(f32) * lambda_k1[0].astype(f32)))
           - jnp.exp(jnp.sum(lambda_q2[0].astype(f32) * lambda_k2[0].astype(f32)))
           + LAMBDA_INIT)
    logit_bound = (1.01 * math.sqrt(HEAD_DIM) * jnp.max(jnp.abs(q_norm_g[0].astype(f32)))
                   * jnp.max(jnp.abs(k_norm_g[0].astype(f32))))
    p = {
        "g1": norm1_g[0].reshape(1, D_MODEL),
        "win": _regroup_w_in(w_in[0]).astype(bf16),
        "qg": jnp.tile(q_norm_g[0], 2).reshape(1, HEAD_WIDTH),
        "kg": jnp.tile(k_norm_g[0], 2).reshape(1, HEAD_WIDTH),
        "gmat": (group[:, None] == group[None, :]).astype(bf16),
        "lam": lam.reshape(1),
        "bounded": (logit_bound < MAX_UNSHIFTED_LOGIT).astype(jnp.int32).reshape(1),
        "subln": subln_g[0].reshape(V_DIM, 1),
        "wpool": w_pool[0].astype(bf16),
        "pscale": pool_scale[0].reshape(1, POOL_WIDTH),
        "wout": w_out[0].astype(bf16),
        "g2": norm2_g[0].reshape(1, D_MODEL),
        "wup": w_up[0].astype(bf16),
        "cw": conv_w[0],
        "cb": conv_b[0].reshape(1, 2 * D_FF),
        "wdown": w_down[0].astype(bf16),
    }
    return (_layer(x_prompt, p), _layer(x_sample, p))
```

```python
import functools
import math

import jax
import jax.numpy as jnp
from jax import lax
from jax.experimental import pallas as pl
from jax.experimental.pallas import tpu as pltpu

D_MODEL = 1024
N_HEADS = 4
HEAD_DIM = 64
V_DIM = 2 * HEAD_DIM
HEAD_WIDTH = 2 * HEAD_DIM
ATTN_WIDTH = N_HEADS * V_DIM
QK_WIDTH = N_HEADS * 2 * HEAD_DIM
POOL_WINDOWS = (2, 4, 8, 16)
POOL_GROUP_DIM = 128
POOL_WIDTH = len(POOL_WINDOWS) * POOL_GROUP_DIM
IN_WIDTH = 2 * QK_WIDTH + ATTN_WIDTH + POOL_WIDTH
D_FF = 2816
ROPE_THETA = 10000.0
EPS = 1e-6
LAMBDA_INIT = 0.8 - 0.6 * math.exp(-0.3 * 0)
MAX_UNSHIFTED_LOGIT = 30.0

HALO = 8
VMEM_LIMIT = 56 * 1024 * 1024

PRE_ROWS = 1024
PRE_CHUNK = 256
ATTN_Q = 256
ATTN_K = 512
ATTN_UNROLLED_TILES = 128
ATTN_K_UNSHIFTED = 256
ATTN_LOOKAHEAD = 2
MIX_ROWS = 1024
MIX_CHUNK = 256
FFN_ROWS = 1024
FFN_CHUNK = 256
FFN_GATE_WIDTH = 128


def _const_spec(shape):
    return pl.BlockSpec(shape, lambda *_: (0,) * len(shape))


def _rms(x, gain):
    return x * lax.rsqrt(jnp.mean(x * x, axis=-1, keepdims=True) + EPS) * gain


def _group_mean_square(x, gmat):
    sq = x * x
    hi = sq.astype(jnp.bfloat16)
    lo = (sq - hi.astype(jnp.float32)).astype(jnp.bfloat16)
    tot = (jnp.dot(hi, gmat, preferred_element_type=jnp.float32)
           + jnp.dot(lo, gmat, preferred_element_type=jnp.float32))
    return tot * (1.0 / HEAD_DIM)


def _norm_rope(x, gain, gmat, cos, sin_signed, first_half):
    xn = x * lax.rsqrt(_group_mean_square(x, gmat) + EPS) * gain
    rot = jnp.where(first_half,
                    pltpu.roll(xn, HEAD_WIDTH - HEAD_DIM // 2, axis=1),
                    pltpu.roll(xn, HEAD_DIM // 2, axis=1))
    return xn * cos + rot * sin_signed


def _pre_kernel(x_ref, g1_ref, win_ref, qg_ref, kg_ref, gmat_ref, cos_ref, sin_ref,
                qt_ref, k_ref, vt_ref, zp_ref):
    x = x_ref[0]
    h = _rms(x, g1_ref[...]).astype(jnp.bfloat16)
    gmat = gmat_ref[...]
    cos = cos_ref[...]
    sin_signed = sin_ref[...]
    lane = lax.broadcasted_iota(jnp.int32, (1, HEAD_WIDTH), 1)
    first_half = (lane % HEAD_DIM) < (HEAD_DIM // 2)
    row = lax.broadcasted_iota(jnp.int32, (HEAD_WIDTH, 1), 0)
    scale = math.log2(math.e) / math.sqrt(HEAD_DIM)

    def proj(c):
        return jnp.dot(h, win_ref[:, c * PRE_CHUNK:(c + 1) * PRE_CHUNK],
                       preferred_element_type=jnp.float32)

    n_chunks = IN_WIDTH // PRE_CHUNK
    z = proj(0)
    for c in range(n_chunks):
        z_next = proj(c + 1) if c + 1 < n_chunks else None
        if c < N_HEADS:
            q = _norm_rope(z[:, :HEAD_WIDTH], qg_ref[...], gmat, cos, sin_signed, first_half)
            qt = (q * scale).T
            qt_ref[0, c, 0] = jnp.where(row < HEAD_DIM, qt, 0.0).astype(jnp.bfloat16)
            qt_ref[0, c, 1] = jnp.where(row >= HEAD_DIM, qt, 0.0).astype(jnp.bfloat16)
            k = _norm_rope(z[:, HEAD_WIDTH:], kg_ref[...], gmat, cos, sin_signed, first_half)
            k_ref[0, :, c * HEAD_WIDTH:(c + 1) * HEAD_WIDTH] = k.astype(jnp.bfloat16)
        elif c < N_HEADS + N_HEADS // 2:
            for j in range(2):
                hd = 2 * (c - N_HEADS) + j
                vt_ref[0, hd] = z[:, j * V_DIM:(j + 1) * V_DIM].T.astype(jnp.bfloat16)
        else:
            c0 = (c - N_HEADS - N_HEADS // 2) * PRE_CHUNK
            zp_ref[0, :, c0:c0 + PRE_CHUNK] = z
        z = z_next


def _pre_call(x, g1, win, qg, kg, gmat, cos, sin_signed):
    B, S, _ = x.shape
    T = PRE_ROWS
    return pl.pallas_call(
        _pre_kernel,
        grid=(B, S // T),
        in_specs=[
            pl.BlockSpec((1, T, D_MODEL), lambda b, i: (b, i, 0)),
            _const_spec((1, D_MODEL)),
            _const_spec((D_MODEL, IN_WIDTH)),
            _const_spec((1, HEAD_WIDTH)),
            _const_spec((1, HEAD_WIDTH)),
            _const_spec((HEAD_WIDTH, HEAD_WIDTH)),
            pl.BlockSpec((T, HEAD_WIDTH), lambda b, i: (i, 0)),
            pl.BlockSpec((T, HEAD_WIDTH), lambda b, i: (i, 0)),
        ],
        out_specs=[
            pl.BlockSpec((1, N_HEADS, 2, HEAD_WIDTH, T), lambda b, i: (b, 0, 0, 0, i)),
            pl.BlockSpec((1, T, QK_WIDTH), lambda b, i: (b, i, 0)),
            pl.BlockSpec((1, N_HEADS, V_DIM, T), lambda b, i: (b, 0, 0, i)),
            pl.BlockSpec((1, T, POOL_WIDTH), lambda b, i: (b, i, 0)),
        ],
        out_shape=[
            jax.ShapeDtypeStruct((B, N_HEADS, 2, HEAD_WIDTH, S), jnp.bfloat16),
            jax.ShapeDtypeStruct((B, S, QK_WIDTH), jnp.bfloat16),
            jax.ShapeDtypeStruct((B, N_HEADS, V_DIM, S), jnp.bfloat16),
            jax.ShapeDtypeStruct((B, S, POOL_WIDTH), jnp.float32),
        ],
        compiler_params=pltpu.CompilerParams(
            dimension_semantics=("parallel", "parallel"), vmem_limit_bytes=VMEM_LIMIT),
        name="pre",
    )(x, g1, win, qg, kg, gmat, cos, sin_signed)


def _attn_kernel(lam_ref, bounded_ref, qt_ref, k_ref, vt_ref, g_ref, o_ref,
                 m_ref, l_ref, acc_ref):
    seq_len = k_ref.shape[1]
    q_per_step = o_ref.shape[1] // ATTN_Q
    lam = lam_ref[0]

    def finish(qi, acc, l):
        o = acc[0] / l[0] - lam * (acc[1] / l[1])
        o = o * lax.rsqrt(jnp.mean(o * o, axis=0, keepdims=True) + EPS) * g_ref[...]
        o_ref[0, pl.ds(qi * ATTN_Q, ATTN_Q), :] = (o * (1.0 - LAMBDA_INIT)).T.astype(o_ref.dtype)

    def q_tile(qi, mp):
        return qt_ref[0, 0, mp, :, pl.ds(qi * ATTN_Q, ATTN_Q)]

    @pl.when(bounded_ref[0] != 0)
    def _():
        tk = ATTN_K_UNSHIFTED
        n_kt = seq_len // tk
        work = [(qi, t) for qi in range(q_per_step) for t in range(n_kt)]

        def scores(qi, t):
            kt = k_ref[0, pl.ds(t * tk, tk), :]
            return [jnp.dot(kt, q_tile(qi, mp), preferred_element_type=jnp.float32)
                    for mp in range(2)]

        ahead = ATTN_LOOKAHEAD
        pending = [scores(*w) for w in work[:ahead]]
        for n, (qi, t) in enumerate(work):
            if n + ahead < len(work):
                pending.append(scores(*work[n + ahead]))
            s = pending.pop(0)
            if t == 0:
                l8 = [None, None]
                acc = [None, None]
            vt = vt_ref[0, 0, :, pl.ds(t * tk, tk)]
            for mp in range(2):
                p = jnp.exp2(s[mp])
                psum = jnp.sum(p.reshape(tk // 8, 8, ATTN_Q), axis=0)
                pv = jnp.dot(vt, p.astype(jnp.bfloat16), preferred_element_type=jnp.float32)
                l8[mp] = psum if t == 0 else l8[mp] + psum
                acc[mp] = pv if t == 0 else acc[mp] + pv
            if t == n_kt - 1:
                finish(qi, acc, [jnp.sum(x, axis=0, keepdims=True) for x in l8])

    @pl.when(bounded_ref[0] == 0)
    def _():
        for qi in range(q_per_step):
            m_ref[...] = jnp.full(m_ref.shape, -jnp.inf, jnp.float32)
            l_ref[...] = jnp.zeros(l_ref.shape, jnp.float32)
            acc_ref[...] = jnp.zeros(acc_ref.shape, jnp.float32)

            def body(j, carry):
                start = pl.multiple_of(j * ATTN_K, ATTN_K)
                kt = k_ref[0, pl.ds(start, ATTN_K), :]
                vt = vt_ref[0, 0, :, pl.ds(start, ATTN_K)]
                for mp in range(2):
                    s = jnp.dot(kt, q_tile(qi, mp), preferred_element_type=jnp.float32)
                    m_prev = m_ref[mp]
                    m_new = jnp.maximum(m_prev, jnp.max(s, axis=0, keepdims=True))
                    p = jnp.exp2(s - m_new)
                    alpha = jnp.exp2(m_prev - m_new)
                    l_ref[mp] = alpha * l_ref[mp] + jnp.sum(p, axis=0, keepdims=True)
                    acc_ref[mp] = alpha * acc_ref[mp] + jnp.dot(
                        vt, p.astype(jnp.bfloat16), preferred_element_type=jnp.float32)
                    m_ref[mp] = m_new
                return carry

            lax.fori_loop(0, seq_len // ATTN_K, body, 0)
            finish(qi, [acc_ref[0], acc_ref[1]], [l_ref[0], l_ref[1]])


def _attn_call(lam, bounded, qt, k, vt, subln_col):
    B, S, _ = k.shape
    q_tiles = min(S // ATTN_Q, max(1, ATTN_UNROLLED_TILES // (S // ATTN_K_UNSHIFTED)))
    q_rows = ATTN_Q * q_tiles
    return pl.pallas_call(
        _attn_kernel,
        grid=(B, N_HEADS, S // q_rows),
        in_specs=[
            pl.BlockSpec(memory_space=pltpu.SMEM),
            pl.BlockSpec(memory_space=pltpu.SMEM),
            pl.BlockSpec((1, 1, 2, HEAD_WIDTH, q_rows), lambda b, h, i: (b, h, 0, 0, i)),
            pl.BlockSpec((1, S, HEAD_WIDTH), lambda b, h, i: (b, 0, h)),
            pl.BlockSpec((1, 1, V_DIM, S), lambda b, h, i: (b, h, 0, 0)),
            _const_spec((V_DIM, 1)),
        ],
        out_specs=pl.BlockSpec((1, q_rows, V_DIM), lambda b, h, i: (b, i, h)),
        out_shape=jax.ShapeDtypeStruct((B, S, ATTN_WIDTH), jnp.bfloat16),
        scratch_shapes=[
            pltpu.VMEM((2, 1, ATTN_Q), jnp.float32),
            pltpu.VMEM((2, 1, ATTN_Q), jnp.float32),
            pltpu.VMEM((2, V_DIM, ATTN_Q), jnp.float32),
        ],
        compiler_params=pltpu.CompilerParams(
            dimension_semantics=("parallel", "parallel", "parallel"),
            vmem_limit_bytes=VMEM_LIMIT),
        name="attn",
    )(lam, bounded, qt, k, vt, subln_col)


def _mix_kernel(x_ref, oa_ref, zp_ref, zprev_ref, znext_ref, wpool_ref, pscale_ref, wout_ref,
                y_ref, ext_ref):
    T = MIX_ROWS
    i = pl.program_id(1)
    n_i = pl.num_programs(1)
    seq_len = n_i * T
    ext_ref[pl.ds(0, HALO), :] = jnp.where(i > 0, zprev_ref[0], 0.0)
    ext_ref[pl.ds(HALO, T), :] = zp_ref[0]
    ext_ref[pl.ds(HALO + T, HALO), :] = jnp.where(i < n_i - 1, znext_ref[0], 0.0)
    R = MIX_CHUNK
    ext_rows = R + 2 * HALO

    def window_sum(e, w):
        tot = e + pltpu.roll(e, 1, axis=0)
        v = 2
        while v < w:
            tot = pltpu.roll(tot, v // 2, axis=0) + pltpu.roll(tot, ext_rows - v // 2, axis=0)
            v *= 2
        return tot

    def pooled_group(r, g):
        w = POOL_WINDOWS[g]
        c0 = g * POOL_GROUP_DIM
        pos = i * T + r * R + lax.broadcasted_iota(jnp.int32, (R, 1), 0)
        e = ext_ref[pl.ds(r * R, ext_rows), c0:c0 + POOL_GROUP_DIM]
        tot = window_sum(e, w)[HALO:HALO + R]
        lo = jnp.maximum(pos - w // 2, 0)
        hi = jnp.minimum(pos + w // 2 - 1, seq_len - 1)
        cnt = (hi - lo + 1).astype(jnp.float32)
        pg = tot / cnt - e[HALO:HALO + R]
        pw = jnp.dot(pg.astype(jnp.bfloat16), wpool_ref[g], preferred_element_type=jnp.float32)
        return (pw * pscale_ref[:, c0:c0 + POOL_GROUP_DIM]).astype(jnp.bfloat16)

    n_groups = len(POOL_WINDOWS)
    n_chunks = T // R
    cols_per_piece = D_MODEL // n_groups
    o_pool = jnp.concatenate([pooled_group(0, g) for g in range(n_groups)], axis=-1)
    for r in range(n_chunks):
        rows = pl.ds(r * R, R)
        oa = oa_ref[0, rows, :]
        next_groups = []
        for j in range(n_groups):
            cols = pl.ds(j * cols_per_piece, cols_per_piece)
            piece = (jnp.dot(oa, wout_ref[:ATTN_WIDTH, cols], preferred_element_type=jnp.float32)
                     + jnp.dot(o_pool, wout_ref[ATTN_WIDTH:, cols],
                               preferred_element_type=jnp.float32))
            if r + 1 < n_chunks:
                next_groups.append(pooled_group(r + 1, j))
            y_ref[0, rows, cols] = x_ref[0, rows, cols] + piece
        if r + 1 < n_chunks:
            o_pool = jnp.concatenate(next_groups, axis=-1)


def _mix_call(x, o_attn, zp, wpool, pscale, wout):
    B, S, _ = x.shape
    T = MIX_ROWS
    nb = T // HALO
    last = S // HALO - 1
    return pl.pallas_call(
        _mix_kernel,
        grid=(B, S // T),
        in_specs=[
            pl.BlockSpec((1, T, D_MODEL), lambda b, i: (b, i, 0)),
            pl.BlockSpec((1, T, ATTN_WIDTH), lambda b, i: (b, i, 0)),
            pl.BlockSpec((1, T, POOL_WIDTH), lambda b, i: (b, i, 0)),
            pl.BlockSpec((1, HALO, POOL_WIDTH), lambda b, i: (b, jnp.maximum(i * nb - 1, 0), 0)),
            pl.BlockSpec((1, HALO, POOL_WIDTH), lambda b, i: (b, jnp.minimum((i + 1) * nb, last), 0)),
            _const_spec((len(POOL_WINDOWS), POOL_GROUP_DIM, POOL_GROUP_DIM)),
            _const_spec((1, POOL_WIDTH)),
            _const_spec((ATTN_WIDTH + POOL_WIDTH, D_MODEL)),
        ],
        out_specs=pl.BlockSpec((1, T, D_MODEL), lambda b, i: (b, i, 0)),
        out_shape=jax.ShapeDtypeStruct((B, S, D_MODEL), jnp.float32),
        scratch_shapes=[pltpu.VMEM((T + 2 * HALO, POOL_WIDTH), jnp.float32)],
        compiler_params=pltpu.CompilerParams(
            dimension_semantics=("parallel", "parallel"), vmem_limit_bytes=VMEM_LIMIT),
        name="mix",
    )(x, o_attn, zp, zp, zp, wpool, pscale, wout)


def _ffn_kernel(x_ref, xprev_ref, xnext_ref, g2_ref, wup_ref, cw_ref, cb_ref, wdown_ref,
                y_ref, h_ref, act_ref):
    T = FFN_ROWS
    i = pl.program_id(1)
    n_i = pl.num_programs(1)
    g2 = g2_ref[...]
    h_ref[pl.ds(0, HALO), :] = _rms(jnp.where(i > 0, xprev_ref[0], 0.0), g2).astype(jnp.bfloat16)
    h_ref[pl.ds(HALO, T), :] = _rms(x_ref[0], g2).astype(jnp.bfloat16)
    h_ref[pl.ds(HALO + T, HALO), :] = _rms(
        jnp.where(i < n_i - 1, xnext_ref[0], 0.0), g2).astype(jnp.bfloat16)
    h = h_ref[...]

    def conv(u, c0, width):
        w = cw_ref[:, c0:c0 + width]
        return (u[HALO - 1:HALO - 1 + T] * w[0:1] + u[HALO:HALO + T] * w[1:2]
                + u[HALO + 1:HALO + 1 + T] * w[2:3] + cb_ref[:, c0:c0 + width])

    def up(c):
        c0 = c * FFN_CHUNK
        return (jnp.dot(h, wup_ref[:, c0:c0 + FFN_CHUNK], preferred_element_type=jnp.float32),
                jnp.dot(h, wup_ref[:, D_FF + c0:D_FF + c0 + FFN_CHUNK],
                        preferred_element_type=jnp.float32))

    n_chunks = D_FF // FFN_CHUNK
    u = up(0)
    for c in range(n_chunks):
        u_next = up(c + 1) if c + 1 < n_chunks else None
        for lane0 in range(0, FFN_CHUNK, FFN_GATE_WIDTH):
            c0 = c * FFN_CHUNK + lane0
            lanes = slice(lane0, lane0 + FFN_GATE_WIDTH)
            half = 0.5 * conv(u[0][:, lanes], c0, FFN_GATE_WIDTH)
            val = conv(u[1][:, lanes], D_FF + c0, FFN_GATE_WIDTH)
            act_ref[:, c0:c0 + FFN_GATE_WIDTH] = (
                (half + half * jnp.tanh(half)) * val).astype(jnp.bfloat16)
        u = u_next
    y_ref[0] = x_ref[0] + jnp.dot(act_ref[...], wdown_ref[...],
                                  preferred_element_type=jnp.float32)


def _ffn_call(x, g2, wup, cw, cb, wdown):
    B, S, _ = x.shape
    T = FFN_ROWS
    nb = T // HALO
    last = S // HALO - 1
    single = pl.Buffered(1)
    return pl.pallas_call(
        _ffn_kernel,
        grid=(B, S // T),
        in_specs=[
            pl.BlockSpec((1, T, D_MODEL), lambda b, i: (b, i, 0)),
            pl.BlockSpec((1, HALO, D_MODEL), lambda b, i: (b, jnp.maximum(i * nb - 1, 0), 0)),
            pl.BlockSpec((1, HALO, D_MODEL), lambda b, i: (b, jnp.minimum((i + 1) * nb, last), 0)),
            _const_spec((1, D_MODEL)),
            pl.BlockSpec((D_MODEL, 2 * D_FF), lambda b, i: (0, 0), pipeline_mode=single),
            _const_spec((3, 2 * D_FF)),
            _const_spec((1, 2 * D_FF)),
            pl.BlockSpec((D_FF, D_MODEL), lambda b, i: (0, 0), pipeline_mode=single),
        ],
        out_specs=pl.BlockSpec((1, T, D_MODEL), lambda b, i: (b, i, 0)),
        out_shape=jax.ShapeDtypeStruct((B, S, D_MODEL), jnp.float32),
        scratch_shapes=[
            pltpu.VMEM((T + 2 * HALO, D_MODEL), jnp.bfloat16),
            pltpu.VMEM((T, D_FF), jnp.bfloat16),
        ],
        compiler_params=pltpu.CompilerParams(
            dimension_semantics=("parallel", "parallel"), vmem_limit_bytes=VMEM_LIMIT),
        name="ffn",
    )(x, x, x, g2, wup, cw, cb, wdown)


def _rope_tables(seq_len):
    inv = ROPE_THETA ** (-jnp.arange(0, HEAD_DIM, 2, dtype=jnp.float32) / HEAD_DIM)
    ang = jnp.arange(seq_len, dtype=jnp.float32)[:, None] * inv[None, :]
    ang = jnp.concatenate([ang, ang], axis=-1)
    cos = jnp.cos(ang)
    sin = jnp.sin(ang)
    half = HEAD_DIM // 2
    sin_signed = jnp.concatenate([-sin[:, :half], sin[:, half:]], axis=-1)
    return jnp.tile(cos, (1, 2)), jnp.tile(sin_signed, (1, 2))


def _regroup_w_in(w):
    cols = []
    for hd in range(N_HEADS):
        lo = hd * HEAD_WIDTH
        cols += [w[:, lo:lo + HEAD_WIDTH], w[:, QK_WIDTH + lo:QK_WIDTH + lo + HEAD_WIDTH]]
    cols.append(w[:, 2 * QK_WIDTH:])
    return jnp.concatenate(cols, axis=1)


def _layer(x, p):
    B, S, _ = x.shape
    cos, sin_signed = _rope_tables(S)
    qt, k, vt, zp = _pre_call(x, p["g1"], p["win"], p["qg"], p["kg"], p["gmat"], cos, sin_signed)
    o_attn = _attn_call(p["lam"], p["bounded"], qt, k, vt, p["subln"])
    x1 = _mix_call(x, o_attn, zp, p["wpool"], p["pscale"], p["wout"])
    return _ffn_call(x1, p["g2"], p["wup"], p["cw"], p["cb"], p["wdown"])


def kernel(x_prompt, x_sample, norm1_g, w_in, q_norm_g, k_norm_g, lambda_q1, lambda_k1,
           lambda_q2, lambda_k2, subln_g, w_pool, pool_scale, w_out, norm2_g, w_up, conv_w,
           conv_b, w_down):
    f32 = jnp.float32
    bf16 = jnp.bfloat16
    group = jnp.arange(HEAD_WIDTH) // HEAD_DIM
    lam = (jnp.exp(jnp.sum(lambda_q1[0].astype(f32) * lambda_k1[0].astype(f32)))
           - jnp.exp(jnp.sum(lambda_q2[0].astype(f32) * lambda_k2[0].astype(f32)))
           + LAMBDA_INIT)
    logit_bound = (1.01 * math.sqrt(HEAD_DIM) * jnp.max(jnp.abs(q_norm_g[0].astype(f32)))
                   * jnp.max(jnp.abs(k_norm_g[0].astype(f32))))
    p = {
        "g1": norm1_g[0].reshape(1, D_MODEL),
        "win": _regroup_w_in(w_in[0]).astype(bf16),
        "qg": jnp.tile(q_norm_g[0], 2).reshape(1, HEAD_WIDTH),
        "kg": jnp.tile(k_norm_g[0], 2).reshape(1, HEAD_WIDTH),
        "gmat": (group[:, None] == group[None, :]).astype(bf16),
        "lam": lam.reshape(1),
        "bounded": (logit_bound < MAX_UNSHIFTED_LOGIT).astype(jnp.int32).reshape(1),
        "subln": subln_g[0].reshape(V_DIM, 1),
        "wpool": w_pool[0].astype(bf16),
        "pscale": pool_scale[0].reshape(1, POOL_WIDTH),
        "wout": w_out[0].astype(bf16),
        "g2": norm2_g[0].reshape(1, D_MODEL),
        "wup": w_up[0].astype(bf16),
        "cw": conv_w[0],
        "cb": conv_b[0].reshape(1, 2 * D_FF),
        "wdown": w_down[0].astype(bf16),
    }
    return (_layer(x_prompt, p), _layer(x_sample, p))
```
